```python
import jax, jax.numpy as jnp
from jax import lax
import numpy as np

D_MODEL = 1024
BATCH = 8
SEQ = 4096
DEPTH = 4

N_BRANCHES = 4
MIX_W = D_MODEL // 4
N_GROUPS = 4
GROUP_W = MIX_W // N_GROUPS
POOL_WINDOWS = (2, 4, 8, 16)
SHORT_CONV_K = 3
CONFORMER_CONV_K = 31
CHUNK = 128
COL_POOL = 0
COL_SCONV = MIX_W
COL_CONF = 4 * MIX_W
COL_GMLP = 6 * MIX_W
COL_GATE = 8 * MIX_W
IN_COLS = COL_GATE + N_BRANCHES * D_MODEL
MEM_LEN = 256
N_XA_HEADS = 4
XA_HEAD_DIM = D_MODEL // N_XA_HEADS
N_EXPERTS = 16
N_EXPERT_GROUPS = 4
EXPERTS_PER_GROUP = N_EXPERTS // N_EXPERT_GROUPS
GROUP_SCORE_TOPK = 2
TOP_K = 2
D_EXPERT = D_MODEL // 4
ALPHA = (2 * DEPTH) ** 0.25
BETA = (8 * DEPTH) ** -0.25
LN_EPS = 1e-5

kernel_name = 'hybrid_gated_pool_conv_gmlp_xattn_grouped_moe'


def layer_norm(x, g, b):
    xf = x.astype(jnp.float32)
    mu = xf.mean(-1, keepdims=True)
    var = jnp.square(xf - mu).mean(-1, keepdims=True)
    y = (xf - mu) * lax.rsqrt(var + LN_EPS)
    return (y * g.astype(jnp.float32) + b.astype(jnp.float32)).astype(x.dtype)


def causal_depthwise_conv(z, w):
    k_width, channels = w.shape
    return lax.conv_general_dilated(
        z, w.astype(z.dtype)[:, None, :], window_strides=(1,),
        padding=[(k_width - 1, 0)], dimension_numbers=('NWC', 'WIO', 'NWC'),
        feature_group_count=channels)


def causal_multiscale_pool(z):
    zf = z.astype(jnp.float32)
    cs = jnp.cumsum(zf, axis=1)
    count = jnp.arange(1, z.shape[1] + 1, dtype=jnp.float32)[None, :, None]
    outs = []
    for g, win in enumerate(POOL_WINDOWS):
        sl = slice(g * GROUP_W, (g + 1) * GROUP_W)
        c = cs[..., sl]
        lag = jnp.pad(c[:, :-win], ((0, 0), (win, 0), (0, 0)))
        outs.append((c - lag) / jnp.minimum(count, win) - zf[..., sl])
    return jnp.concatenate(outs, axis=-1).astype(z.dtype)


def hybrid_mixer(h, w_in, b_gate, pool_w, pool_scale, sc_conv_w, cf_conv_w, cf_ln_g, cf_ln_b,
                 gm_ln_g, gm_ln_b, gm_ws, gm_bs, w_branch, w_out):
    bsz, seq, _ = h.shape
    p = h @ w_in
    a = causal_multiscale_pool(p[..., COL_POOL:COL_POOL + MIX_W])
    a = jnp.einsum('bsgc,gcd->bsgd', a.reshape(bsz, seq, N_GROUPS, GROUP_W), pool_w)
    a = a.reshape(bsz, seq, MIX_W) * pool_scale
    xs, bg, cg = jnp.split(p[..., COL_SCONV:COL_CONF], 3, axis=-1)
    bb = bg * causal_depthwise_conv(cg * xs, sc_conv_w)
    ga, gb = jnp.split(p[..., COL_CONF:COL_GMLP], 2, axis=-1)
    c = causal_depthwise_conv(ga * jax.nn.sigmoid(gb), cf_conv_w)
    c = jax.nn.silu(layer_norm(c, cf_ln_g, cf_ln_b))
    u, v = jnp.split(jax.nn.gelu(p[..., COL_GMLP:COL_GATE]), 2, axis=-1)
    v = layer_norm(v, gm_ln_g, gm_ln_b).reshape(bsz, seq // CHUNK, CHUNK, N_GROUPS, GROUP_W)
    ws = gm_ws * jnp.tril(jnp.ones((CHUNK, CHUNK), gm_ws.dtype))
    sv = jnp.einsum('gts,bnsgc->bntgc', ws, v) + jnp.swapaxes(gm_bs, 0, 1)[:, :, None]
    d = u * sv.reshape(bsz, seq, MIX_W)
    gates = jax.nn.sigmoid(p[..., COL_GATE:] + b_gate).reshape(bsz, seq, N_BRANCHES, D_MODEL)
    merged = None
    for i, br in enumerate((a, bb, c, d)):
        term = gates[:, :, i, :] * (br @ w_branch[i])
        merged = term if merged is None else merged + term
    return merged @ w_out


def memory_cross_attention(h, mem, wq, wkv, wo):
    bsz, seq, _ = h.shape
    m = mem.shape[1]
    q = (h @ wq).reshape(bsz, seq, N_XA_HEADS, XA_HEAD_DIM)
    k, v = jnp.split(mem @ wkv, 2, axis=-1)
    k = k.reshape(bsz, m, N_XA_HEADS, XA_HEAD_DIM)
    v = v.reshape(bsz, m, N_XA_HEADS, XA_HEAD_DIM)
    s = jnp.einsum('bshd,bmhd->bhsm', q, k).astype(jnp.float32) * (XA_HEAD_DIM ** -0.5)
    probs = jax.nn.softmax(s, axis=-1).astype(v.dtype)
    o = jnp.einsum('bhsm,bmhd->bshd', probs, v).reshape(bsz, seq, D_MODEL)
    return o @ wo


def grouped_moe(h, router_w, router_b, w_gu, w_down):
    bsz, seq, d = h.shape
    hf = h.reshape(bsz * seq, d)
    logits = hf.astype(jnp.float32) @ router_w.astype(jnp.float32) + router_b.astype(jnp.float32)
    scores = jax.nn.sigmoid(logits)
    grp = scores.reshape(-1, N_EXPERT_GROUPS, EXPERTS_PER_GROUP)
    group_score = lax.top_k(grp, GROUP_SCORE_TOPK)[0].sum(-1)
    sel = jnp.argmax(group_score, axis=-1)
    in_group = (jnp.arange(N_EXPERTS) // EXPERTS_PER_GROUP)[None, :] == sel[:, None]
    vals, idx = lax.top_k(jnp.where(in_group, scores, -1.0), TOP_K)
    wts = vals / vals.sum(-1, keepdims=True)
    comb = jnp.einsum('tk,tke->te', wts, jax.nn.one_hot(idx, N_EXPERTS, dtype=jnp.float32)).astype(h.dtype)
    y = None
    for e in range(N_EXPERTS):
        g, u = jnp.split(hf @ w_gu[e], 2, axis=-1)
        term = comb[:, e:e + 1] * ((jax.nn.silu(g) * u) @ w_down[e])
        y = term if y is None else y + term
    return y.reshape(bsz, seq, d)


def setup_inputs(seed: int = 0) -> dict:
    key = jax.random.key(seed)
    ks = jax.random.split(key, 32)
    L = DEPTH

    def nrm(k, shape, scale):
        return jax.random.normal(k, shape, jnp.float32) * scale

    return {
        'x': nrm(ks[0], (BATCH, SEQ, D_MODEL), 1.0),
        'mem': nrm(ks[1], (BATCH, MEM_LEN, D_MODEL), 1.0),
        'w_in': nrm(ks[2], (L, D_MODEL, IN_COLS), D_MODEL ** -0.5),
        'b_gate': nrm(ks[3], (L, N_BRANCHES * D_MODEL), 0.02),
        'pool_w': nrm(ks[4], (L, N_GROUPS, GROUP_W, GROUP_W), GROUP_W ** -0.5),
        'pool_scale': 1.0 + nrm(ks[5], (L, MIX_W), 0.02),
        'sc_conv_w': nrm(ks[6], (L, SHORT_CONV_K, MIX_W), SHORT_CONV_K ** -0.5),
        'cf_conv_w': nrm(ks[7], (L, CONFORMER_CONV_K, MIX_W), CONFORMER_CONV_K ** -0.5),
        'cf_ln_g': 1.0 + nrm(ks[8], (L, MIX_W), 0.02),
        'cf_ln_b': nrm(ks[9], (L, MIX_W), 0.02),
        'gm_ln_g': 1.0 + nrm(ks[10], (L, MIX_W), 0.02),
        'gm_ln_b': nrm(ks[11], (L, MIX_W), 0.02),
        'gm_ws': nrm(ks[12], (L, N_GROUPS, CHUNK, CHUNK), CHUNK ** -0.5),
        'gm_bs': 1.0 + nrm(ks[13], (L, N_GROUPS, CHUNK), 0.02),
        'w_branch': nrm(ks[14], (L, N_BRANCHES, MIX_W, D_MODEL), MIX_W ** -0.5),
        'w_out': nrm(ks[15], (L, D_MODEL, D_MODEL), D_MODEL ** -0.5 * BETA),
        'ln1_g': 1.0 + nrm(ks[16], (L, D_MODEL), 0.02),
        'ln1_b': nrm(ks[17], (L, D_MODEL), 0.02),
        'xa_wq': nrm(ks[18], (L, D_MODEL, D_MODEL), D_MODEL ** -0.5),
        'xa_wkv': jnp.concatenate([nrm(ks[19], (L, D_MODEL, D_MODEL), D_MODEL ** -0.5),
                                   nrm(ks[20], (L, D_MODEL, D_MODEL), D_MODEL ** -0.5 * BETA)], axis=-1),
        'xa_wo': nrm(ks[21], (L, D_MODEL, D_MODEL), D_MODEL ** -0.5 * BETA),
        'ln2_g': 1.0 + nrm(ks[22], (L, D_MODEL), 0.02),
        'ln2_b': nrm(ks[23], (L, D_MODEL), 0.02),
        'router_w': nrm(ks[24], (D_MODEL, N_EXPERTS), D_MODEL ** -0.5),
        'router_b': nrm(ks[25], (N_EXPERTS,), 0.01),
        'moe_w_gu': nrm(ks[26], (L, N_EXPERTS, D_MODEL, 2 * D_EXPERT), D_MODEL ** -0.5),
        'moe_w_down': nrm(ks[27], (L, N_EXPERTS, D_EXPERT, D_MODEL), D_EXPERT ** -0.5 * BETA),
        'ln3_g': 1.0 + nrm(ks[28], (L, D_MODEL), 0.02),
        'ln3_b': nrm(ks[29], (L, D_MODEL), 0.02),
    }


def reference(x, mem, w_in, b_gate, pool_w, pool_scale, sc_conv_w, cf_conv_w, cf_ln_g, cf_ln_b,
              gm_ln_g, gm_ln_b, gm_ws, gm_bs, w_branch, w_out, ln1_g, ln1_b, xa_wq, xa_wkv, xa_wo,
              ln2_g, ln2_b, router_w, router_b, moe_w_gu, moe_w_down, ln3_g, ln3_b):
    for l in range(DEPTH):
        mix = hybrid_mixer(x, w_in[l], b_gate[l], pool_w[l], pool_scale[l], sc_conv_w[l], cf_conv_w[l],
                           cf_ln_g[l], cf_ln_b[l], gm_ln_g[l], gm_ln_b[l], gm_ws[l], gm_bs[l],
                           w_branch[l], w_out[l])
        x = layer_norm(ALPHA * x + mix, ln1_g[l], ln1_b[l])
        xa = memory_cross_attention(x, mem, xa_wq[l], xa_wkv[l], xa_wo[l])
        x = layer_norm(ALPHA * x + xa, ln2_g[l], ln2_b[l])
        ff = grouped_moe(x, router_w, router_b, moe_w_gu[l], moe_w_down[l])
        x = layer_norm(ALPHA * x + ff, ln3_g[l], ln3_b[l])
    return x
```

```python
import functools

import jax
import jax.numpy as jnp
from jax import lax
from jax.experimental import pallas as pl
from jax.experimental.pallas import tpu as pltpu

F32 = jnp.float32
BF16 = jnp.bfloat16

D_MODEL = 1024
DEPTH = 4
MIX_W = 256
N_GROUPS = 4
GROUP_W = 64
SHORT_CONV_K = 3
CONFORMER_CONV_K = 31
CHUNK = 128
COL_SCONV = MIX_W
COL_CONF = 4 * MIX_W
COL_GMLP = 6 * MIX_W
COL_GATE = 8 * MIX_W
N_BRANCHES = 4
IN_COLS = COL_GATE + N_BRANCHES * D_MODEL
MEM_LEN = 256
N_XA_HEADS = 4
XA_HEAD_DIM = D_MODEL // N_XA_HEADS
N_EXPERTS = 16
N_EXPERT_GROUPS = 4
EXPERTS_PER_GROUP = 4
D_EXPERT = 256
ALPHA = (2 * DEPTH) ** 0.25
LN_EPS = 1e-5

V7X_SUBLANES = 8
V7X_LANES = 128
V7X_VMEM_BYTES = 64 * 1024 * 1024

HALO = 32
SC_HALO = V7X_SUBLANES
CONV_ROWS = 64
ROUTER_LANE_STRIDE = 32


def _layer_norm(y, g, b):
    mu = jnp.mean(y, axis=-1, keepdims=True)
    d = y - mu
    var = jnp.mean(d * d, axis=-1, keepdims=True)
    return d * lax.rsqrt(var + LN_EPS) * g + b


def _dot(a, b):
    return jnp.dot(a, b, preferred_element_type=F32)


def _const_spec(shape, index):
    return pl.BlockSpec(shape, lambda *_: index, pipeline_mode=pl.Buffered(1))


def _vmem_limit(nbytes):
    return int(min(V7X_VMEM_BYTES - 4 * 1024 * 1024, nbytes))


def _kv_kernel(mem_ref, wkv_ref, k_ref, v_ref):
    kv = _dot(mem_ref[...].astype(BF16), wkv_ref[...])
    k_ref[...] = kv[:, :D_MODEL].astype(BF16)
    v_ref[...] = kv[:, D_MODEL:].astype(BF16)


def _kv_call(mem, wkv_bf16):
    bsz = mem.shape[0]
    out = jax.ShapeDtypeStruct((DEPTH, bsz, MEM_LEN, D_MODEL), BF16)
    return pl.pallas_call(
        _kv_kernel,
        grid=(DEPTH, bsz),
        in_specs=[
            pl.BlockSpec((None, MEM_LEN, D_MODEL), lambda l, b: (b, 0, 0)),
            pl.BlockSpec((None, D_MODEL, 2 * D_MODEL), lambda l, b: (l, 0, 0)),
        ],
        out_specs=[
            pl.BlockSpec((None, None, MEM_LEN, D_MODEL), lambda l, b: (l, b, 0, 0)),
            pl.BlockSpec((None, None, MEM_LEN, D_MODEL), lambda l, b: (l, b, 0, 0)),
        ],
        out_shape=[out, out],
        compiler_params=pltpu.CompilerParams(dimension_semantics=("arbitrary", "arbitrary")),
        name="mem_kv",
    )(mem, wkv_bf16)


def _mixer_kernel(x_ref, w_in_ref, b_gate_ref, pool_w_ref, pool_scale_ref, sc_w_ref, cf_w_ref,
                  cf_g_ref, cf_b_ref, gm_g_ref, gm_b_ref, gm_ws_ref, gm_bias_ref, w_br_ref, w_out_ref,
                  ln_g_ref, ln_b_ref, o_ref,
                  pool_e, pool_a, pool_b, pool_c, sc_e, cf_e, br_c, br_d):
    s = pl.program_id(1)
    tile = x_ref.shape[0]

    @pl.when(s == 0)
    def _():
        pool_e[0:HALO, :] = jnp.zeros((HALO, MIX_W), F32)
        sc_e[0:SC_HALO, :] = jnp.zeros((SC_HALO, MIX_W), F32)
        cf_e[0:HALO, :] = jnp.zeros((HALO, MIX_W), F32)

    x = x_ref[...]
    xb = x.astype(BF16)

    def proj(lo, hi):
        return _dot(xb, w_in_ref[:, lo:hi])

    lane = lax.broadcasted_iota(jnp.int32, (1, MIX_W), 1)
    lane_group = lane // GROUP_W

    z = proj(0, MIX_W)
    end = tile + HALO
    pool_e[HALO:end, :] = z
    pool_a[8:end, :] = pool_e[8:end, :] + pool_e[7:end - 1, :]
    pool_b[16:end, :] = pool_a[16:end, :] + pool_a[14:end - 2, :]
    pool_c[24:end, :] = pool_b[24:end, :] + pool_b[20:end - 4, :]
    s16 = pool_c[HALO:end, :] + pool_c[HALO - 8:end - 8, :]
    ssum = jnp.where(lane_group == 0, pool_a[HALO:end, :],
                     jnp.where(lane_group == 1, pool_b[HALO:end, :],
                               jnp.where(lane_group == 2, pool_c[HALO:end, :], s16)))
    win = jnp.where(lane_group == 0, 2.0,
                    jnp.where(lane_group == 1, 4.0, jnp.where(lane_group == 2, 8.0, 16.0))).astype(F32)
    count = (s * tile + lax.broadcasted_iota(jnp.int32, (tile, 1), 0) + 1).astype(F32)
    a = ssum / jnp.minimum(count, win) - z
    pool_e[0:HALO, :] = pool_e[tile:end, :]
    a = _dot(a.astype(BF16), pool_w_ref[...]) * pool_scale_ref[...]

    pb = proj(COL_SCONV, COL_CONF)
    xs, bg, cg = pb[:, :MIX_W], pb[:, MIX_W:2 * MIX_W], pb[:, 2 * MIX_W:]
    u = cg * xs
    sc_e[SC_HALO:SC_HALO + tile, :] = u
    conv = (sc_w_ref[0:1, :] * sc_e[SC_HALO - 2:SC_HALO - 2 + tile, :]
            + sc_w_ref[1:2, :] * sc_e[SC_HALO - 1:SC_HALO - 1 + tile, :]
            + sc_w_ref[2:3, :] * u)
    bb = bg * conv
    sc_e[0:SC_HALO, :] = sc_e[tile:tile + SC_HALO, :]

    pc = proj(COL_CONF, COL_GMLP)
    cf_e[HALO:end, :] = pc[:, :MIX_W] * jax.nn.sigmoid(pc[:, MIX_W:])
    first = HALO - (CONFORMER_CONV_K - 1)
    for r in range(tile // CONV_ROWS):
        base = r * CONV_ROWS + first
        acc = cf_w_ref[0:1, :] * cf_e[base:base + CONV_ROWS, :]
        for k in range(1, CONFORMER_CONV_K):
            acc = acc + cf_w_ref[k:k + 1, :] * cf_e[base + k:base + k + CONV_ROWS, :]
        cn = _layer_norm(acc, cf_g_ref[...], cf_b_ref[...])
        br_c[r * CONV_ROWS:(r + 1) * CONV_ROWS, :] = (cn * jax.nn.sigmoid(cn)).astype(BF16)
    cf_e[0:HALO, :] = cf_e[tile:end, :]

    pd = jax.nn.gelu(proj(COL_GMLP, COL_GATE))
    gu = pd[:, :MIX_W]
    gv = _layer_norm(pd[:, MIX_W:], gm_g_ref[...], gm_b_ref[...]).astype(BF16)
    row = lax.broadcasted_iota(jnp.int32, (CHUNK, CHUNK), 0)
    col = lax.broadcasted_iota(jnp.int32, (CHUNK, CHUNK), 1)
    tril = row >= col
    ws = [jnp.where(tril, gm_ws_ref[g], 0.0).astype(BF16) for g in range(N_GROUPS)]
    for c in range(tile // CHUNK):
        vc = gv[c * CHUNK:(c + 1) * CHUNK, :]
        sv = gm_bias_ref[...]
        for g in range(N_GROUPS):
            sv = sv + _dot(ws[g], jnp.where(lane_group == g, vc, jnp.zeros_like(vc)))
        br_d[c * CHUNK:(c + 1) * CHUNK, :] = (gu[c * CHUNK:(c + 1) * CHUNK, :] * sv).astype(BF16)

    branches = (a.astype(BF16), bb.astype(BF16), br_c[...], br_d[...])
    merged = None
    for i, br in enumerate(branches):
        lo = COL_GATE + i * D_MODEL
        gate = jax.nn.sigmoid(proj(lo, lo + D_MODEL) + b_gate_ref[:, i * D_MODEL:(i + 1) * D_MODEL])
        term = gate * _dot(br, w_br_ref[i])
        merged = term if merged is None else merged + term
    mix = _dot(merged.astype(BF16), w_out_ref[...])
    o_ref[...] = _layer_norm(ALPHA * x + mix, ln_g_ref[...], ln_b_ref[...])


def _mixer_call(x, l, w, tile):
    bsz, seq, _ = x.shape
    rows = tile + HALO
    vec = lambda n: _const_spec((None, 1, n), (l, 0, 0))
    in_specs = [
        pl.BlockSpec((None, tile, D_MODEL), lambda b, s: (b, s, 0)),
        _const_spec((None, D_MODEL, IN_COLS), (l, 0, 0)),
        vec(N_BRANCHES * D_MODEL),
        _const_spec((None, MIX_W, MIX_W), (l, 0, 0)),
        vec(MIX_W),
        _const_spec((None, SHORT_CONV_K, MIX_W), (l, 0, 0)),
        _const_spec((None, CONFORMER_CONV_K, MIX_W), (l, 0, 0)),
        vec(MIX_W), vec(MIX_W), vec(MIX_W), vec(MIX_W),
        _const_spec((None, N_GROUPS, CHUNK, CHUNK), (l, 0, 0, 0)),
        _const_spec((None, CHUNK, MIX_W), (l, 0, 0)),
        _const_spec((None, N_BRANCHES, MIX_W, D_MODEL), (l, 0, 0, 0)),
        _const_spec((None, D_MODEL, D_MODEL), (l, 0, 0)),
        vec(D_MODEL), vec(D_MODEL),
    ]
    scratch = [
        pltpu.VMEM((rows, MIX_W), F32), pltpu.VMEM((rows, MIX_W), F32),
        pltpu.VMEM((rows, MIX_W), F32), pltpu.VMEM((rows, MIX_W), F32),
        pltpu.VMEM((tile + 2 * SC_HALO, MIX_W), F32),
        pltpu.VMEM((rows, MIX_W), F32),
        pltpu.VMEM((tile, MIX_W), BF16), pltpu.VMEM((tile, MIX_W), BF16),
    ]
    weight_bytes = 2 * (D_MODEL * IN_COLS + N_BRANCHES * MIX_W * D_MODEL + D_MODEL * D_MODEL + MIX_W * MIX_W)
    io_bytes = 2 * 2 * tile * D_MODEL * 4
    work_bytes = 12 * tile * D_MODEL * 4
    return pl.pallas_call(
        _mixer_kernel,
        grid=(bsz, seq // tile),
        in_specs=in_specs,
        out_specs=pl.BlockSpec((None, tile, D_MODEL), lambda b, s: (b, s, 0)),
        out_shape=jax.ShapeDtypeStruct(x.shape, F32),
        scratch_shapes=scratch,
        compiler_params=pltpu.CompilerParams(
            dimension_semantics=("arbitrary", "arbitrary"),
            vmem_limit_bytes=_vmem_limit(weight_bytes + io_bytes + work_bytes)),
        name="mixer",
    )(x, w["w_in"], w["b_gate"], w["pool_w"], w["pool_scale"], w["sc_conv_w"], w["cf_conv_w"],
      w["cf_ln_g"], w["cf_ln_b"], w["gm_ln_g"], w["gm_ln_b"], w["gm_ws"], w["gm_bias"],
      w["w_branch"], w["w_out"], w["ln1_g"], w["ln1_b"])


def _xattn_kernel(x_ref, k_ref, v_ref, wq_ref, wo_ref, ln_g_ref, ln_b_ref, o_ref):
    x = x_ref[...]
    q = _dot(x.astype(BF16), wq_ref[...]).astype(BF16)
    heads = []
    for h in range(N_XA_HEADS):
        sl = slice(h * XA_HEAD_DIM, (h + 1) * XA_HEAD_DIM)
        sc = lax.dot_general(q[:, sl], k_ref[:, sl], (((1,), (1,)), ((), ())),
                             preferred_element_type=F32) * (XA_HEAD_DIM ** -0.5)
        sc = sc - jnp.max(sc, axis=-1, keepdims=True)
        e = jnp.exp(sc)
        probs = e / jnp.sum(e, axis=-1, keepdims=True)
        heads.append(_dot(probs.astype(BF16), v_ref[:, sl]).astype(BF16))
    o = jnp.concatenate(heads, axis=-1)
    xa = _dot(o, wo_ref[...])
    o_ref[...] = _layer_norm(ALPHA * x + xa, ln_g_ref[...], ln_b_ref[...])


def _xattn_call(x, k, v, l, w, tile):
    bsz, seq, _ = x.shape
    vec = lambda n: _const_spec((None, 1, n), (l, 0, 0))
    weight_bytes = 2 * 2 * D_MODEL * D_MODEL + 2 * 2 * 2 * MEM_LEN * D_MODEL
    io_bytes = 2 * 2 * tile * D_MODEL * 4
    work_bytes = 8 * tile * D_MODEL * 4
    return pl.pallas_call(
        _xattn_kernel,
        grid=(bsz, seq // tile),
        in_specs=[
            pl.BlockSpec((None, tile, D_MODEL), lambda b, s: (b, s, 0)),
            pl.BlockSpec((None, None, MEM_LEN, D_MODEL), lambda b, s: (l, b, 0, 0)),
            pl.BlockSpec((None, None, MEM_LEN, D_MODEL), lambda b, s: (l, b, 0, 0)),
            _const_spec((None, D_MODEL, D_MODEL), (l, 0, 0)),
            _const_spec((None, D_MODEL, D_MODEL), (l, 0, 0)),
            vec(D_MODEL), vec(D_MODEL),
        ],
        out_specs=pl.BlockSpec((None, tile, D_MODEL), lambda b, s: (b, s, 0)),
        out_shape=jax.ShapeDtypeStruct(x.shape, F32),
        compiler_params=pltpu.CompilerParams(
            dimension_semantics=("arbitrary", "arbitrary"),
            vmem_limit_bytes=_vmem_limit(weight_bytes + io_bytes + work_bytes)),
        name="xattn",
    )(x, k, v, w["xa_wq"], w["xa_wo"], w["ln2_g"], w["ln2_b"])


def _moe_kernel(x_ref, rw_ref, rb_ref, wgu_ref, wdn_ref, ln_g_ref, ln_b_ref, o_ref):
    x = x_ref[...]
    tile = x.shape[0]
    xb = x.astype(BF16)

    x_lo = (x - xb.astype(F32)).astype(BF16)
    logits = _dot(xb, rw_ref[0]) + _dot(x_lo, rw_ref[0]) + _dot(xb, rw_ref[1]) + rb_ref[...]
    scores = jax.nn.sigmoid(logits)
    lane = lax.broadcasted_iota(jnp.int32, (1, V7X_LANES), 1)
    valid = lane < N_EXPERT_GROUPS
    sj = [scores if j == 0 else pltpu.roll(scores, V7X_LANES - j * ROUTER_LANE_STRIDE, axis=1)
          for j in range(EXPERTS_PER_GROUP)]
    hi01, lo01 = jnp.maximum(sj[0], sj[1]), jnp.minimum(sj[0], sj[1])
    hi23, lo23 = jnp.maximum(sj[2], sj[3]), jnp.minimum(sj[2], sj[3])
    top1 = jnp.maximum(hi01, hi23)
    top2 = jnp.maximum(jnp.minimum(hi01, hi23), jnp.maximum(lo01, lo23))
    gscore = jnp.where(valid, top1 + top2, -1.0)
    gmax = jnp.max(gscore, axis=-1, keepdims=True)
    sel = jnp.min(jnp.where(gscore == gmax, lane, V7X_LANES), axis=-1, keepdims=True)
    in_sel = lane == sel
    picked = []
    for j in range(EXPERTS_PER_GROUP):
        rank = jnp.zeros_like(scores)
        for i in range(EXPERTS_PER_GROUP):
            if i < j:
                rank = rank + (sj[i] >= sj[j]).astype(F32)
            elif i > j:
                rank = rank + (sj[i] > sj[j]).astype(F32)
        picked.append(jnp.where(in_sel & (rank < 2.0), sj[j], 0.0))
    denom = jnp.sum(picked[0] + picked[1] + picked[2] + picked[3], axis=-1, keepdims=True)
    comb = [p / denom for p in picked]

    y = None
    for g in range(N_EXPERT_GROUPS):
        gu = _dot(xb, wgu_ref[g])
        gate, up = gu[:, :EXPERTS_PER_GROUP * D_EXPERT], gu[:, EXPERTS_PER_GROUP * D_EXPERT:]
        h = gate * jax.nn.sigmoid(gate) * up
        hs = []
        for j in range(EXPERTS_PER_GROUP):
            c = jnp.sum(jnp.where(lane == g, comb[j], 0.0), axis=-1, keepdims=True)
            hs.append((h[:, j * D_EXPERT:(j + 1) * D_EXPERT] * c).astype(BF16))
        term = _dot(jnp.concatenate(hs, axis=-1), wdn_ref[g])
        y = term if y is None else y + term
    o_ref[...] = _layer_norm(ALPHA * x + y, ln_g_ref[...], ln_b_ref[...])


def _moe_call(x2d, l, w, tile):
    n = x2d.shape[0]
    vec = lambda m: _const_spec((None, 1, m), (l, 0, 0))
    hid = EXPERTS_PER_GROUP * D_EXPERT
    weight_bytes = 2 * N_EXPERT_GROUPS * (D_MODEL * 2 * hid + hid * D_MODEL)
    io_bytes = 2 * 2 * tile * D_MODEL * 4
    work_bytes = 8 * tile * D_MODEL * 4
    return pl.pallas_call(
        _moe_kernel,
        grid=(n // tile,),
        in_specs=[
            pl.BlockSpec((tile, D_MODEL), lambda t: (t, 0)),
            _const_spec((2, D_MODEL, V7X_LANES), (0, 0, 0)),
            _const_spec((1, V7X_LANES), (0, 0)),
            _const_spec((None, N_EXPERT_GROUPS, D_MODEL, 2 * hid), (l, 0, 0, 0)),
            _const_spec((None, N_EXPERT_GROUPS, hid, D_MODEL), (l, 0, 0, 0)),
            vec(D_MODEL), vec(D_MODEL),
        ],
        out_specs=pl.BlockSpec((tile, D_MODEL), lambda t: (t, 0)),
        out_shape=jax.ShapeDtypeStruct(x2d.shape, F32),
        compiler_params=pltpu.CompilerParams(
            dimension_semantics=("arbitrary",),
            vmem_limit_bytes=_vmem_limit(weight_bytes + io_bytes + work_bytes)),
        name="moe",
    )(x2d, w["router_w"], w["router_b"], w["moe_wgu"], w["moe_wdn"], w["ln3_g"], w["ln3_b"])


def _prepare(p):
    depth = p["w_in"].shape[0]
    row = lambda a: a.reshape(depth, 1, -1)
    w = {k: p[k].astype(BF16) for k in ("w_in", "w_branch", "w_out", "xa_wq", "xa_wkv", "xa_wo")}
    for k in ("b_gate", "pool_scale", "cf_ln_g", "cf_ln_b", "gm_ln_g", "gm_ln_b",
              "ln1_g", "ln1_b", "ln2_g", "ln2_b", "ln3_g", "ln3_b"):
        w[k] = row(p[k])
    for k in ("sc_conv_w", "cf_conv_w", "gm_ws"):
        w[k] = p[k]
    eye = jnp.eye(N_GROUPS, dtype=F32)
    w["pool_w"] = jnp.einsum("lgcd,gh->lgchd", p["pool_w"], eye).reshape(depth, MIX_W, MIX_W).astype(BF16)
    w["gm_bias"] = jnp.repeat(jnp.swapaxes(p["gm_bs"], 1, 2), GROUP_W, axis=2)
    lanes = (jnp.arange(N_EXPERTS) % EXPERTS_PER_GROUP) * ROUTER_LANE_STRIDE + jnp.arange(N_EXPERTS) // EXPERTS_PER_GROUP
    rw = jnp.zeros((D_MODEL, V7X_LANES), F32).at[:, lanes].set(p["router_w"].astype(F32))
    rw_hi = rw.astype(BF16)
    rw_lo = (rw - rw_hi.astype(F32)).astype(BF16)
    w["router_w"] = jnp.stack([rw_hi, rw_lo])
    w["router_b"] = jnp.zeros((1, V7X_LANES), F32).at[0, lanes].set(p["router_b"].astype(F32))
    wgu = p["moe_w_gu"].reshape(depth, N_EXPERT_GROUPS, EXPERTS_PER_GROUP, D_MODEL, 2, D_EXPERT)
    w["moe_wgu"] = jnp.transpose(wgu, (0, 1, 3, 4, 2, 5)).reshape(
        depth, N_EXPERT_GROUPS, D_MODEL, 2 * EXPERTS_PER_GROUP * D_EXPERT).astype(BF16)
    w["moe_wdn"] = p["moe_w_down"].reshape(
        depth, N_EXPERT_GROUPS, EXPERTS_PER_GROUP * D_EXPERT, D_MODEL).astype(BF16)
    return w


def kernel(x, mem, w_in, b_gate, pool_w, pool_scale, sc_conv_w, cf_conv_w, cf_ln_g, cf_ln_b, gm_ln_g, gm_ln_b, gm_ws, gm_bs, w_branch, w_out, ln1_g, ln1_b, xa_wq, xa_wkv, xa_wo, ln2_g, ln2_b, router_w, router_b, moe_w_gu, moe_w_down, ln3_g, ln3_b):
    params = dict(w_in=w_in, b_gate=b_gate, pool_w=pool_w, pool_scale=pool_scale, sc_conv_w=sc_conv_w,
                  cf_conv_w=cf_conv_w, cf_ln_g=cf_ln_g, cf_ln_b=cf_ln_b, gm_ln_g=gm_ln_g, gm_ln_b=gm_ln_b,
                  gm_ws=gm_ws, gm_bs=gm_bs, w_branch=w_branch, w_out=w_out, ln1_g=ln1_g, ln1_b=ln1_b,
                  xa_wq=xa_wq, xa_wkv=xa_wkv, xa_wo=xa_wo, ln2_g=ln2_g, ln2_b=ln2_b, router_w=router_w,
                  router_b=router_b, moe_w_gu=moe_w_gu, moe_w_down=moe_w_down, ln3_g=ln3_g, ln3_b=ln3_b)
    w = _prepare(params)
    bsz, seq, d = x.shape
    k_all, v_all = _kv_call(mem, w["xa_wkv"])
    tile = 512
    for l in range(DEPTH):
        x = _mixer_call(x, l, w, tile)
        x = _xattn_call(x, k_all, v_all, l, w, tile)
        x = _moe_call(x.reshape(bsz * seq, d), l, w, tile).reshape(bsz, seq, d)
    return x
```

```python
import functools

import jax
import jax.numpy as jnp
from jax import lax
from jax.experimental import pallas as pl
from jax.experimental.pallas import tpu as pltpu

F32 = jnp.float32
BF16 = jnp.bfloat16

D_MODEL = 1024
DEPTH = 4
MIX_W = 256
N_GROUPS = 4
GROUP_W = 64
SHORT_CONV_K = 3
CONFORMER_CONV_K = 31
CHUNK = 128
COL_SCONV = MIX_W
COL_CONF = 4 * MIX_W
COL_GMLP = 6 * MIX_W
COL_GATE = 8 * MIX_W
N_BRANCHES = 4
IN_COLS = COL_GATE + N_BRANCHES * D_MODEL
MEM_LEN = 256
N_XA_HEADS = 4
XA_HEAD_DIM = D_MODEL // N_XA_HEADS
N_EXPERTS = 16
N_EXPERT_GROUPS = 4
EXPERTS_PER_GROUP = 4
D_EXPERT = 256
ALPHA = (2 * DEPTH) ** 0.25
LN_EPS = 1e-5

V7X_SUBLANES = 8
V7X_LANES = 128
V7X_VMEM_BYTES = 64 * 1024 * 1024

HALO = 32
SC_HALO = V7X_SUBLANES
CONV_ROWS = 64
ROUTER_LANE_STRIDE = 32
RANK_BITS = 16


def _layer_norm(y, g, b):
    mu = jnp.mean(y, axis=-1, keepdims=True)
    d = y - mu
    var = jnp.mean(d * d, axis=-1, keepdims=True)
    return d * lax.rsqrt(var + LN_EPS) * g + b


def _dot(a, b):
    return jnp.dot(a, b, preferred_element_type=F32)


CHUNKS = D_MODEL // V7X_LANES


def _store_token_major(ref, value):
    tile = value.shape[0]
    for k in range(CHUNKS):
        ref[pl.ds(k, tile, stride=CHUNKS), :] = value[:, k * V7X_LANES:(k + 1) * V7X_LANES]


def _load_token_major(ref):
    tile = ref.shape[0] // CHUNKS
    return jnp.concatenate([ref[pl.ds(k, tile, stride=CHUNKS), :] for k in range(CHUNKS)], axis=1)


def _const_spec(shape, index):
    return pl.BlockSpec(shape, lambda *_: index, pipeline_mode=pl.Buffered(1))


def _vmem_limit(nbytes):
    return int(min(V7X_VMEM_BYTES - 4 * 1024 * 1024, nbytes))


def _kv_kernel(mem_ref, wkv_ref, k_ref, v_ref):
    kv = _dot(mem_ref[...].astype(BF16), wkv_ref[...])
    k_ref[...] = kv[:, :D_MODEL].astype(BF16)
    v_ref[...] = kv[:, D_MODEL:].astype(BF16)


def _kv_call(mem, wkv_bf16):
    bsz = mem.shape[0]
    out = jax.ShapeDtypeStruct((DEPTH, bsz, MEM_LEN, D_MODEL), BF16)
    return pl.pallas_call(
        _kv_kernel,
        grid=(DEPTH, bsz),
        in_specs=[
            pl.BlockSpec((None, MEM_LEN, D_MODEL), lambda l, b: (b, 0, 0)),
            pl.BlockSpec((None, D_MODEL, 2 * D_MODEL), lambda l, b: (l, 0, 0)),
        ],
        out_specs=[
            pl.BlockSpec((None, None, MEM_LEN, D_MODEL), lambda l, b: (l, b, 0, 0)),
            pl.BlockSpec((None, None, MEM_LEN, D_MODEL), lambda l, b: (l, b, 0, 0)),
        ],
        out_shape=[out, out],
        compiler_params=pltpu.CompilerParams(dimension_semantics=("arbitrary", "arbitrary")),
        name="mem_kv",
    )(mem, wkv_bf16)


def _mixer_kernel(x_ref, w_in_ref, b_gate_ref, pool_w_ref, pool_scale_ref, sc_w_ref, cf_w_ref,
                  cf_g_ref, cf_b_ref, gm_g_ref, gm_b_ref, gm_ws_ref, gm_bias_ref, w_br_ref, w_out_ref,
                  ln_g_ref, ln_b_ref, o_ref,
                  pool_e, pool_a, pool_b, pool_c, sc_e, cf_e, br_c, br_d):
    s = pl.program_id(1)
    tile = x_ref.shape[0]

    @pl.when(s == 0)
    def _():
        pool_e[0:HALO, :] = jnp.zeros((HALO, MIX_W), F32)
        sc_e[0:SC_HALO, :] = jnp.zeros((SC_HALO, MIX_W), F32)
        cf_e[0:HALO, :] = jnp.zeros((HALO, MIX_W), F32)

    x = x_ref[...]
    xb = x.astype(BF16)

    def proj(lo, hi):
        return _dot(xb, w_in_ref[:, lo:hi])

    lane = lax.broadcasted_iota(jnp.int32, (1, MIX_W), 1)
    lane_group = lane // GROUP_W

    z = proj(0, MIX_W)
    end = tile + HALO
    pool_e[HALO:end, :] = z
    pool_a[8:end, :] = pool_e[8:end, :] + pool_e[7:end - 1, :]
    pool_b[16:end, :] = pool_a[16:end, :] + pool_a[14:end - 2, :]
    pool_c[24:end, :] = pool_b[24:end, :] + pool_b[20:end - 4, :]
    s16 = pool_c[HALO:end, :] + pool_c[HALO - 8:end - 8, :]
    ssum = jnp.where(lane_group == 0, pool_a[HALO:end, :],
                     jnp.where(lane_group == 1, pool_b[HALO:end, :],
                               jnp.where(lane_group == 2, pool_c[HALO:end, :], s16)))
    win = jnp.where(lane_group == 0, 2.0,
                    jnp.where(lane_group == 1, 4.0, jnp.where(lane_group == 2, 8.0, 16.0))).astype(F32)
    count = (s * tile + lax.broadcasted_iota(jnp.int32, (tile, 1), 0) + 1).astype(F32)
    a = ssum / jnp.minimum(count, win) - z
    pool_e[0:HALO, :] = pool_e[tile:end, :]
    a = _dot(a.astype(BF16), pool_w_ref[...]) * pool_scale_ref[...]

    pb = proj(COL_SCONV, COL_CONF)
    xs, bg, cg = pb[:, :MIX_W], pb[:, MIX_W:2 * MIX_W], pb[:, 2 * MIX_W:]
    u = cg * xs
    sc_e[SC_HALO:SC_HALO + tile, :] = u
    conv = (sc_w_ref[0:1, :] * sc_e[SC_HALO - 2:SC_HALO - 2 + tile, :]
            + sc_w_ref[1:2, :] * sc_e[SC_HALO - 1:SC_HALO - 1 + tile, :]
            + sc_w_ref[2:3, :] * u)
    bb = bg * conv
    sc_e[0:SC_HALO, :] = sc_e[tile:tile + SC_HALO, :]

    pc = proj(COL_CONF, COL_GMLP)
    cf_e[HALO:end, :] = pc[:, :MIX_W] * jax.nn.sigmoid(pc[:, MIX_W:])
    first = HALO - (CONFORMER_CONV_K - 1)
    for r in range(tile // CONV_ROWS):
        base = r * CONV_ROWS + first
        acc = cf_w_ref[0:1, :] * cf_e[base:base + CONV_ROWS, :]
        for k in range(1, CONFORMER_CONV_K):
            acc = acc + cf_w_ref[k:k + 1, :] * cf_e[base + k:base + k + CONV_ROWS, :]
        cn = _layer_norm(acc, cf_g_ref[...], cf_b_ref[...])
        br_c[r * CONV_ROWS:(r + 1) * CONV_ROWS, :] = (cn * jax.nn.sigmoid(cn)).astype(BF16)
    cf_e[0:HALO, :] = cf_e[tile:end, :]

    pd = jax.nn.gelu(proj(COL_GMLP, COL_GATE))
    gu = pd[:, :MIX_W]
    gv = _layer_norm(pd[:, MIX_W:], gm_g_ref[...], gm_b_ref[...]).astype(BF16)
    row = lax.broadcasted_iota(jnp.int32, (CHUNK, CHUNK), 0)
    col = lax.broadcasted_iota(jnp.int32, (CHUNK, CHUNK), 1)
    tril = row >= col
    ws = [jnp.where(tril, gm_ws_ref[g], 0.0).astype(BF16) for g in range(N_GROUPS)]
    for c in range(tile // CHUNK):
        vc = gv[c * CHUNK:(c + 1) * CHUNK, :]
        sv = gm_bias_ref[...]
        for g in range(N_GROUPS):
            sv = sv + _dot(ws[g], jnp.where(lane_group == g, vc, jnp.zeros_like(vc)))
        br_d[c * CHUNK:(c + 1) * CHUNK, :] = (gu[c * CHUNK:(c + 1) * CHUNK, :] * sv).astype(BF16)

    branches = (a.astype(BF16), bb.astype(BF16), br_c[...], br_d[...])
    merged = None
    for i, br in enumerate(branches):
        lo = COL_GATE + i * D_MODEL
        gate = jax.nn.sigmoid(proj(lo, lo + D_MODEL) + b_gate_ref[:, i * D_MODEL:(i + 1) * D_MODEL])
        term = gate * _dot(br, w_br_ref[i])
        merged = term if merged is None else merged + term
    mix = _dot(merged.astype(BF16), w_out_ref[...])
    o_ref[...] = _layer_norm(ALPHA * x + mix, ln_g_ref[...], ln_b_ref[...])


def _mixer_call(x, l, w, tile):
    bsz, seq, _ = x.shape
    rows = tile + HALO
    vec = lambda n: _const_spec((None, 1, n), (l, 0, 0))
    in_specs = [
        pl.BlockSpec((None, tile, D_MODEL), lambda b, s: (b, s, 0)),
        _const_spec((None, D_MODEL, IN_COLS), (l, 0, 0)),
        vec(N_BRANCHES * D_MODEL),
        _const_spec((None, MIX_W, MIX_W), (l, 0, 0)),
        vec(MIX_W),
        _const_spec((None, SHORT_CONV_K, MIX_W), (l, 0, 0)),
        _const_spec((None, CONFORMER_CONV_K, MIX_W), (l, 0, 0)),
        vec(MIX_W), vec(MIX_W), vec(MIX_W), vec(MIX_W),
        _const_spec((None, N_GROUPS, CHUNK, CHUNK), (l, 0, 0, 0)),
        _const_spec((None, CHUNK, MIX_W), (l, 0, 0)),
        _const_spec((None, N_BRANCHES, MIX_W, D_MODEL), (l, 0, 0, 0)),
        _const_spec((None, D_MODEL, D_MODEL), (l, 0, 0)),
        vec(D_MODEL), vec(D_MODEL),
    ]
    scratch = [
        pltpu.VMEM((rows, MIX_W), F32), pltpu.VMEM((rows, MIX_W), F32),
        pltpu.VMEM((rows, MIX_W), F32), pltpu.VMEM((rows, MIX_W), F32),
        pltpu.VMEM((tile + 2 * SC_HALO, MIX_W), F32),
        pltpu.VMEM((rows, MIX_W), F32),
        pltpu.VMEM((tile, MIX_W), BF16), pltpu.VMEM((tile, MIX_W), BF16),
    ]
    weight_bytes = 2 * (D_MODEL * IN_COLS + N_BRANCHES * MIX_W * D_MODEL + D_MODEL * D_MODEL + MIX_W * MIX_W)
    io_bytes = 2 * 2 * tile * D_MODEL * 4
    work_bytes = 12 * tile * D_MODEL * 4
    return pl.pallas_call(
        _mixer_kernel,
        grid=(bsz, seq // tile),
        in_specs=in_specs,
        out_specs=pl.BlockSpec((None, tile, D_MODEL), lambda b, s: (b, s, 0)),
        out_shape=jax.ShapeDtypeStruct(x.shape, F32),
        scratch_shapes=scratch,
        compiler_params=pltpu.CompilerParams(
            dimension_semantics=("arbitrary", "arbitrary"),
            vmem_limit_bytes=_vmem_limit(weight_bytes + io_bytes + work_bytes)),
        name="mixer",
    )(x, w["w_in"], w["b_gate"], w["pool_w"], w["pool_scale"], w["sc_conv_w"], w["cf_conv_w"],
      w["cf_ln_g"], w["cf_ln_b"], w["gm_ln_g"], w["gm_ln_b"], w["gm_ws"], w["gm_bias"],
      w["w_branch"], w["w_out"], w["ln1_g"], w["ln1_b"])


def _router_scores(x, rw_ref, rb_ref):
    x_hi = x.astype(BF16)
    x_lo = (x - x_hi.astype(F32)).astype(BF16)
    logits = _dot(x_hi, rw_ref[0]) + _dot(x_lo, rw_ref[0]) + _dot(x_hi, rw_ref[1]) + rb_ref[...]
    return jax.nn.sigmoid(logits)


def _expert_slots(scores):
    return [scores if j == 0 else pltpu.roll(scores, V7X_LANES - j * ROUTER_LANE_STRIDE, axis=1)
            for j in range(EXPERTS_PER_GROUP)]


def _top2_weights(sj, keep):
    picked = []
    for j in range(EXPERTS_PER_GROUP):
        rank = jnp.zeros_like(sj[j])
        for i in range(EXPERTS_PER_GROUP):
            if i < j:
                rank = rank + (sj[i] >= sj[j]).astype(F32)
            elif i > j:
                rank = rank + (sj[i] > sj[j]).astype(F32)
        picked.append(jnp.where(keep & (rank < 2.0), sj[j], 0.0))
    denom = jnp.sum(picked[0] + picked[1] + picked[2] + picked[3], axis=-1, keepdims=True)
    return [p / denom for p in picked]


def _xattn_kernel(x_ref, k_ref, v_ref, wq_ref, wo_ref, ln_g_ref, ln_b_ref, rw_ref, rb_ref,
                  o_ref, pos_ref, cnt_ref, base_ref):
    @pl.when((pl.program_id(0) == 0) & (pl.program_id(1) == 0))
    def _():
        base_ref[...] = jnp.zeros_like(base_ref)

    x = x_ref[...]
    tile = x.shape[0]
    q = _dot(x.astype(BF16), wq_ref[...]).astype(BF16)
    heads = []
    for h in range(N_XA_HEADS):
        sl = slice(h * XA_HEAD_DIM, (h + 1) * XA_HEAD_DIM)
        sc = lax.dot_general(q[:, sl], k_ref[:, sl], (((1,), (1,)), ((), ())),
                             preferred_element_type=F32) * (XA_HEAD_DIM ** -0.5)
        sc = sc - jnp.max(sc, axis=-1, keepdims=True)
        e = jnp.exp(sc)
        probs = e / jnp.sum(e, axis=-1, keepdims=True)
        heads.append(_dot(probs.astype(BF16), v_ref[:, sl]).astype(BF16))
    o = jnp.concatenate(heads, axis=-1)
    xa = _dot(o, wo_ref[...])
    x2 = _layer_norm(ALPHA * x + xa, ln_g_ref[...], ln_b_ref[...])
    _store_token_major(o_ref, x2)

    sj = _expert_slots(_router_scores(x2, rw_ref, rb_ref))
    lane = lax.broadcasted_iota(jnp.int32, (1, V7X_LANES), 1)
    hi01, lo01 = jnp.maximum(sj[0], sj[1]), jnp.minimum(sj[0], sj[1])
    hi23, lo23 = jnp.maximum(sj[2], sj[3]), jnp.minimum(sj[2], sj[3])
    top1 = jnp.maximum(hi01, hi23)
    top2 = jnp.maximum(jnp.minimum(hi01, hi23), jnp.maximum(lo01, lo23))
    gscore = jnp.where(lane < N_EXPERT_GROUPS, top1 + top2, -1.0)
    gmax = jnp.max(gscore, axis=-1, keepdims=True)
    sel = jnp.min(jnp.where(gscore == gmax, lane, V7X_LANES), axis=-1, keepdims=True)
    onehot = (lane == sel).astype(F32)

    r = lax.broadcasted_iota(jnp.int32, (tile, tile), 0)
    c = lax.broadcasted_iota(jnp.int32, (tile, tile), 1)
    earlier = _dot((r > c).astype(BF16), onehot.astype(BF16))
    base = base_ref[...]
    rank = jnp.sum(onehot * (earlier + base), axis=-1, keepdims=True)
    base = base + jnp.sum(onehot, axis=0, keepdims=True)
    base_ref[...] = base
    code = sel.astype(F32) * float(1 << RANK_BITS) + rank
    code_rows = jnp.transpose(jnp.broadcast_to(code, (tile, V7X_LANES)))
    pos_ref[...] = code_rows[0:1, :].astype(jnp.int32)
    cnt_ref[...] = jnp.broadcast_to(base, (V7X_SUBLANES, V7X_LANES)).astype(jnp.int32)


def _xattn_call(x, k, v, l, w, tile):
    bsz, seq, _ = x.shape
    steps = seq // tile
    vec = lambda n: _const_spec((None, 1, n), (l, 0, 0))
    weight_bytes = 2 * 2 * D_MODEL * D_MODEL + 2 * 2 * 2 * MEM_LEN * D_MODEL
    io_bytes = 2 * 2 * tile * D_MODEL * 4
    work_bytes = 10 * tile * D_MODEL * 4
    return pl.pallas_call(
        _xattn_kernel,
        grid=(bsz, steps),
        in_specs=[
            pl.BlockSpec((None, tile, D_MODEL), lambda b, s: (b, s, 0)),
            pl.BlockSpec((None, None, MEM_LEN, D_MODEL), lambda b, s: (l, b, 0, 0)),
            pl.BlockSpec((None, None, MEM_LEN, D_MODEL), lambda b, s: (l, b, 0, 0)),
            _const_spec((None, D_MODEL, D_MODEL), (l, 0, 0)),
            _const_spec((None, D_MODEL, D_MODEL), (l, 0, 0)),
            vec(D_MODEL), vec(D_MODEL),
            _const_spec((2, D_MODEL, V7X_LANES), (0, 0, 0)),
            _const_spec((1, V7X_LANES), (0, 0)),
        ],
        out_specs=[
            pl.BlockSpec((tile * CHUNKS, V7X_LANES), lambda b, s: (b * steps + s, 0)),
            pl.BlockSpec((None, 1, tile), lambda b, s: (b * steps + s, 0, 0)),
            pl.BlockSpec((V7X_SUBLANES, V7X_LANES), lambda b, s: (0, 0)),
        ],
        out_shape=[
            jax.ShapeDtypeStruct((bsz * seq * CHUNKS, V7X_LANES), F32),
            jax.ShapeDtypeStruct((bsz * steps, 1, tile), jnp.int32),
            jax.ShapeDtypeStruct((V7X_SUBLANES, V7X_LANES), jnp.int32),
        ],
        scratch_shapes=[pltpu.VMEM((1, V7X_LANES), F32)],
        compiler_params=pltpu.CompilerParams(
            dimension_semantics=("arbitrary", "arbitrary"),
            vmem_limit_bytes=_vmem_limit(weight_bytes + io_bytes + work_bytes)),
        name="xattn",
    )(x, k, v, w["xa_wq"], w["xa_wo"], w["ln2_g"], w["ln2_b"], w["router_w"], w["router_b"])


def _token_rows(ref, t):
    return ref.at[pl.ds(pl.multiple_of(t * CHUNKS, CHUNKS), CHUNKS)]


def _group_tiles(cnt_ref, tile):
    shift = tile.bit_length() - 1
    assert tile == 1 << shift
    firsts, total = [], jnp.int32(0)
    for g in range(N_EXPERT_GROUPS):
        firsts.append(total)
        total = total + lax.shift_right_logical(cnt_ref[g] + (tile - 1), shift)
    return firsts, total


def _sorted_row(code, first_row_ref):
    group = lax.shift_right_logical(code, RANK_BITS)
    return first_row_ref[group] + (code & ((1 << RANK_BITS) - 1))


def _dispatch_kernel(cnt_ref, code_ref, x_ref, xs_ref, zero_ref, first_row_ref, sem):
    tile = x_ref.shape[0] // CHUNKS
    tile_rows = tile * CHUNKS
    n_tiles = xs_ref.shape[0] // tile_rows
    firsts, total = _group_tiles(cnt_ref, tile)
    for g in range(N_EXPERT_GROUPS):
        first_row_ref[g] = firsts[g] * tile

    def clear_tile(t):
        fill = pltpu.make_async_copy(
            zero_ref, xs_ref.at[pl.ds(pl.multiple_of(t * tile_rows, tile_rows), tile_rows)], sem)
        fill.start()
        fill.wait()

    @pl.when(pl.program_id(0) == 0)
    def _():
        zero_ref[...] = jnp.zeros_like(zero_ref)
        ends = firsts[1:] + [total]
        for g in range(N_EXPERT_GROUPS):
            @pl.when(cnt_ref[g] > 0)
            def _():
                clear_tile(ends[g] - 1)
        for t in range(n_tiles - N_EXPERT_GROUPS, n_tiles):
            @pl.when(total <= t)
            def _():
                clear_tile(t)

    def token_copy(r, p):
        return pltpu.make_async_copy(_token_rows(x_ref, r), _token_rows(xs_ref, p), sem)

    def issue(r, carry):
        token_copy(r, _sorted_row(code_ref[0, r], first_row_ref)).start()
        return carry

    lax.fori_loop(0, tile, issue, 0, unroll=8)
    for _ in range(tile):
        token_copy(0, 0).wait()


def _dispatch_call(x_tm, code, counts, tile):
    n = x_tm.shape[0] // CHUNKS
    n_tiles = n // tile + N_EXPERT_GROUPS
    return pl.pallas_call(
        _dispatch_kernel,
        grid_spec=pltpu.PrefetchScalarGridSpec(
            num_scalar_prefetch=1,
            grid=(n // tile,),
            in_specs=[
                pl.BlockSpec((None, 1, tile), lambda t, cnt: (t, 0, 0), memory_space=pltpu.SMEM),
                pl.BlockSpec((tile * CHUNKS, V7X_LANES), lambda t, cnt: (t, 0)),
            ],
            out_specs=pl.BlockSpec(memory_space=pl.ANY),
            scratch_shapes=[pltpu.VMEM((tile * CHUNKS, V7X_LANES), F32),
                            pltpu.SMEM((N_EXPERT_GROUPS,), jnp.int32),
                            pltpu.SemaphoreType.DMA],
        ),
        out_shape=jax.ShapeDtypeStruct((n_tiles * tile * CHUNKS, V7X_LANES), F32),
        compiler_params=pltpu.CompilerParams(
            dimension_semantics=("arbitrary",),
            vmem_limit_bytes=_vmem_limit(4 * tile * D_MODEL * 4 + (1 << 20))),
        name="dispatch",
    )(counts, code, x_tm)


def _gather_kernel(cnt_ref, code_ref, xs_ref, o_ref, buf_ref, first_row_ref, sem):
    tile = o_ref.shape[0]
    firsts, _ = _group_tiles(cnt_ref, tile)
    for g in range(N_EXPERT_GROUPS):
        first_row_ref[g] = firsts[g] * tile

    def token_copy(r, p):
        return pltpu.make_async_copy(_token_rows(xs_ref, p), _token_rows(buf_ref, r), sem)

    def issue(r, carry):
        token_copy(r, _sorted_row(code_ref[0, r], first_row_ref)).start()
        return carry

    lax.fori_loop(0, tile, issue, 0, unroll=8)
    for _ in range(tile):
        token_copy(0, 0).wait()
    o_ref[...] = _load_token_major(buf_ref)


def _gather_call(xs, code, counts, n, tile):
    return pl.pallas_call(
        _gather_kernel,
        grid_spec=pltpu.PrefetchScalarGridSpec(
            num_scalar_prefetch=1,
            grid=(n // tile,),
            in_specs=[
                pl.BlockSpec((None, 1, tile), lambda t, cnt: (t, 0, 0), memory_space=pltpu.SMEM),
                pl.BlockSpec(memory_space=pl.ANY),
            ],
            out_specs=pl.BlockSpec((tile, D_MODEL), lambda t, cnt: (t, 0)),
            scratch_shapes=[pltpu.VMEM((tile * CHUNKS, V7X_LANES), F32),
                            pltpu.SMEM((N_EXPERT_GROUPS,), jnp.int32),
                            pltpu.SemaphoreType.DMA],
        ),
        out_shape=jax.ShapeDtypeStruct((n, D_MODEL), F32),
        compiler_params=pltpu.CompilerParams(
            dimension_semantics=("arbitrary",),
            vmem_limit_bytes=_vmem_limit(4 * tile * D_MODEL * 4 + (1 << 20))),
        name="gather",
    )(counts, code, xs)


def _tile_group(step, cnt_ref, tile):
    firsts, total = _group_tiles(cnt_ref, tile)
    group = jnp.int32(0)
    for g in range(1, N_EXPERT_GROUPS):
        group = group + (step >= firsts[g]).astype(jnp.int32)
    return group, step < total


def _moe_kernel(cnt_ref, x_ref, rw_ref, rb_ref, wgu_ref, wdn_ref, ln_g_ref, ln_b_ref, o_ref):
    tile = x_ref.shape[0] // CHUNKS
    _, valid = _tile_group(pl.program_id(0), cnt_ref, tile)

    @pl.when(jnp.logical_not(valid))
    def _():
        o_ref[...] = jnp.zeros_like(o_ref)

    @pl.when(valid)
    def _():
        x = _load_token_major(x_ref)
        xb = x.astype(BF16)
        sj = _expert_slots(_router_scores(x, rw_ref, rb_ref))
        lane = lax.broadcasted_iota(jnp.int32, (1, V7X_LANES), 1)
        comb = _top2_weights(sj, lane == 0)
        hid = EXPERTS_PER_GROUP * D_EXPERT
        gu = _dot(xb, wgu_ref[...])
        gate, up = gu[:, :hid], gu[:, hid:]
        h = gate * jax.nn.sigmoid(gate) * up
        hs = [(h[:, j * D_EXPERT:(j + 1) * D_EXPERT] * comb[j][:, 0:1]).astype(BF16)
              for j in range(EXPERTS_PER_GROUP)]
        y = _dot(jnp.concatenate(hs, axis=-1), wdn_ref[...])
        _store_token_major(o_ref, _layer_norm(ALPHA * x + y, ln_g_ref[...], ln_b_ref[...]))


def _moe_call(xs, counts, l, w, tile):
    hid = EXPERTS_PER_GROUP * D_EXPERT
    n_tiles = xs.shape[0] // (tile * CHUNKS)

    def group_of(t, cnt):
        return _tile_group(t, cnt, tile)[0]

    vec = lambda m: pl.BlockSpec((None, 1, m), lambda t, cnt: (l, 0, 0), pipeline_mode=pl.Buffered(1))
    weight_bytes = 2 * 2 * (D_MODEL * 2 * hid + hid * D_MODEL)
    io_bytes = 2 * 2 * tile * D_MODEL * 4
    work_bytes = 10 * tile * D_MODEL * 4
    return pl.pallas_call(
        _moe_kernel,
        grid_spec=pltpu.PrefetchScalarGridSpec(
            num_scalar_prefetch=1,
            grid=(n_tiles,),
            in_specs=[
                pl.BlockSpec((tile * CHUNKS, V7X_LANES), lambda t, cnt: (t, 0)),
                pl.BlockSpec((None, 2, D_MODEL, V7X_LANES), lambda t, cnt: (group_of(t, cnt), 0, 0, 0)),
                pl.BlockSpec((None, 1, V7X_LANES), lambda t, cnt: (group_of(t, cnt), 0, 0)),
                pl.BlockSpec((None, None, D_MODEL, 2 * hid), lambda t, cnt: (l, group_of(t, cnt), 0, 0)),
                pl.BlockSpec((None, None, hid, D_MODEL), lambda t, cnt: (l, group_of(t, cnt), 0, 0)),
                vec(D_MODEL), vec(D_MODEL),
            ],
            out_specs=pl.BlockSpec((tile * CHUNKS, V7X_LANES), lambda t, cnt: (t, 0)),
        ),
        out_shape=jax.ShapeDtypeStruct(xs.shape, F32),
        compiler_params=pltpu.CompilerParams(
            dimension_semantics=("arbitrary",),
            vmem_limit_bytes=_vmem_limit(weight_bytes + io_bytes + work_bytes)),
        name="moe",
    )(counts, xs, w["router_wg"], w["router_bg"], w["moe_wgu"], w["moe_wdn"], w["ln3_g"], w["ln3_b"])


def _prepare(p):
    depth = p["w_in"].shape[0]
    row = lambda a: a.reshape(depth, 1, -1)
    w = {k: p[k].astype(BF16) for k in ("w_in", "w_branch", "w_out", "xa_wq", "xa_wkv", "xa_wo")}
    for k in ("b_gate", "pool_scale", "cf_ln_g", "cf_ln_b", "gm_ln_g", "gm_ln_b",
              "ln1_g", "ln1_b", "ln2_g", "ln2_b", "ln3_g", "ln3_b"):
        w[k] = row(p[k])
    for k in ("sc_conv_w", "cf_conv_w", "gm_ws"):
        w[k] = p[k]
    eye = jnp.eye(N_GROUPS, dtype=F32)
    w["pool_w"] = jnp.einsum("lgcd,gh->lgchd", p["pool_w"], eye).reshape(depth, MIX_W, MIX_W).astype(BF16)
    w["gm_bias"] = jnp.repeat(jnp.swapaxes(p["gm_bs"], 1, 2), GROUP_W, axis=2)
    lanes = (jnp.arange(N_EXPERTS) % EXPERTS_PER_GROUP) * ROUTER_LANE_STRIDE + jnp.arange(N_EXPERTS) // EXPERTS_PER_GROUP
    rw = jnp.zeros((D_MODEL, V7X_LANES), F32).at[:, lanes].set(p["router_w"].astype(F32))
    rw_hi = rw.astype(BF16)
    rw_lo = (rw - rw_hi.astype(F32)).astype(BF16)
    w["router_w"] = jnp.stack([rw_hi, rw_lo])
    w["router_b"] = jnp.zeros((1, V7X_LANES), F32).at[0, lanes].set(p["router_b"].astype(F32))
    slot = jnp.arange(EXPERTS_PER_GROUP) * ROUTER_LANE_STRIDE
    rwg = jnp.zeros((N_EXPERT_GROUPS, D_MODEL, V7X_LANES), F32).at[:, :, slot].set(
        jnp.transpose(p["router_w"].astype(F32).reshape(D_MODEL, N_EXPERT_GROUPS, EXPERTS_PER_GROUP), (1, 0, 2)))
    rwg_hi = rwg.astype(BF16)
    w["router_wg"] = jnp.stack([rwg_hi, (rwg - rwg_hi.astype(F32)).astype(BF16)], axis=1)
    w["router_bg"] = jnp.zeros((N_EXPERT_GROUPS, 1, V7X_LANES), F32).at[:, 0, slot].set(
        p["router_b"].astype(F32).reshape(N_EXPERT_GROUPS, EXPERTS_PER_GROUP))
    wgu = p["moe_w_gu"].reshape(depth, N_EXPERT_GROUPS, EXPERTS_PER_GROUP, D_MODEL, 2, D_EXPERT)
    w["moe_wgu"] = jnp.transpose(wgu, (0, 1, 3, 4, 2, 5)).reshape(
        depth, N_EXPERT_GROUPS, D_MODEL, 2 * EXPERTS_PER_GROUP * D_EXPERT).astype(BF16)
    w["moe_wdn"] = p["moe_w_down"].reshape(
        depth, N_EXPERT_GROUPS, EXPERTS_PER_GROUP * D_EXPERT, D_MODEL).astype(BF16)
    return w


def kernel(x, mem, w_in, b_gate, pool_w, pool_scale, sc_conv_w, cf_conv_w, cf_ln_g, cf_ln_b, gm_ln_g, gm_ln_b, gm_ws, gm_bs, w_branch, w_out, ln1_g, ln1_b, xa_wq, xa_wkv, xa_wo, ln2_g, ln2_b, router_w, router_b, moe_w_gu, moe_w_down, ln3_g, ln3_b):
    params = dict(w_in=w_in, b_gate=b_gate, pool_w=pool_w, pool_scale=pool_scale, sc_conv_w=sc_conv_w,
                  cf_conv_w=cf_conv_w, cf_ln_g=cf_ln_g, cf_ln_b=cf_ln_b, gm_ln_g=gm_ln_g, gm_ln_b=gm_ln_b,
                  gm_ws=gm_ws, gm_bs=gm_bs, w_branch=w_branch, w_out=w_out, ln1_g=ln1_g, ln1_b=ln1_b,
                  xa_wq=xa_wq, xa_wkv=xa_wkv, xa_wo=xa_wo, ln2_g=ln2_g, ln2_b=ln2_b, router_w=router_w,
                  router_b=router_b, moe_w_gu=moe_w_gu, moe_w_down=moe_w_down, ln3_g=ln3_g, ln3_b=ln3_b)
    w = _prepare(params)
    bsz, seq, d = x.shape
    n = bsz * seq
    k_all, v_all = _kv_call(mem, w["xa_wkv"])
    tile = 512
    assert n <= 1 << RANK_BITS
    for l in range(DEPTH):
        x = _mixer_call(x, l, w, tile)
        x2, code, cnt = _xattn_call(x, k_all, v_all, l, w, tile)
        counts = cnt[0, :N_EXPERT_GROUPS]
        xs = _dispatch_call(x2, code, counts, tile)
        xs = _moe_call(xs, counts, l, w, tile)
        x = _gather_call(xs, code, counts, n, tile).reshape(bsz, seq, d)
    return x
```

```python
import functools

import jax
import jax.numpy as jnp
from jax import lax
from jax.experimental import pallas as pl
from jax.experimental.pallas import tpu as pltpu

F32 = jnp.float32
BF16 = jnp.bfloat16

D_MODEL = 1024
DEPTH = 4
MIX_W = 256
N_GROUPS = 4
GROUP_W = 64
SHORT_CONV_K = 3
CONFORMER_CONV_K = 31
CHUNK = 128
COL_SCONV = MIX_W
COL_CONF = 4 * MIX_W
COL_GMLP = 6 * MIX_W
COL_GATE = 8 * MIX_W
N_BRANCHES = 4
IN_COLS = COL_GATE + N_BRANCHES * D_MODEL
MEM_LEN = 256
N_XA_HEADS = 4
XA_HEAD_DIM = D_MODEL // N_XA_HEADS
N_EXPERTS = 16
N_EXPERT_GROUPS = 4
EXPERTS_PER_GROUP = 4
D_EXPERT = 256
ALPHA = (2 * DEPTH) ** 0.25
LN_EPS = 1e-5

V7X_SUBLANES = 8
V7X_LANES = 128
V7X_VMEM_BYTES = 64 * 1024 * 1024

HALO = 32
SC_HALO = V7X_SUBLANES
GATE_SLAB = 512
ROUTER_LANE_STRIDE = 32
RANK_BITS = 16
ISSUE_UNROLL = 8


def _layer_norm(y, g, b):
    mu = jnp.mean(y, axis=-1, keepdims=True)
    d = y - mu
    var = jnp.mean(d * d, axis=-1, keepdims=True)
    return d * lax.rsqrt(var + LN_EPS) * g + b


def _dot(a, b):
    return jnp.dot(a, b, preferred_element_type=F32)


CHUNKS = D_MODEL // V7X_LANES


def _store_token_major(ref, value):
    tile = value.shape[0]
    for k in range(CHUNKS):
        ref[pl.ds(k, tile, stride=CHUNKS), :] = value[:, k * V7X_LANES:(k + 1) * V7X_LANES]


def _load_token_major(ref):
    tile = ref.shape[0] // CHUNKS
    return jnp.concatenate([ref[pl.ds(k, tile, stride=CHUNKS), :] for k in range(CHUNKS)], axis=1)


def _const_spec(shape, index):
    return pl.BlockSpec(shape, lambda *_: index, pipeline_mode=pl.Buffered(1))


def _vmem_limit(nbytes):
    return int(min(V7X_VMEM_BYTES - 4 * 1024 * 1024, nbytes))


def _kv_kernel(mem_ref, wkv_ref, k_ref, v_ref):
    kv = _dot(mem_ref[...].astype(BF16), wkv_ref[...])
    k_ref[...] = kv[:, :D_MODEL].astype(BF16)
    v_ref[...] = kv[:, D_MODEL:].astype(BF16)


def _kv_call(mem, wkv_bf16):
    bsz = mem.shape[0]
    out = jax.ShapeDtypeStruct((DEPTH, bsz, MEM_LEN, D_MODEL), BF16)
    return pl.pallas_call(
        _kv_kernel,
        grid=(DEPTH, bsz),
        in_specs=[
            pl.BlockSpec((None, MEM_LEN, D_MODEL), lambda l, b: (b, 0, 0)),
            pl.BlockSpec((None, D_MODEL, 2 * D_MODEL), lambda l, b: (l, 0, 0)),
        ],
        out_specs=[
            pl.BlockSpec((None, None, MEM_LEN, D_MODEL), lambda l, b: (l, b, 0, 0)),
            pl.BlockSpec((None, None, MEM_LEN, D_MODEL), lambda l, b: (l, b, 0, 0)),
        ],
        out_shape=[out, out],
        compiler_params=pltpu.CompilerParams(dimension_semantics=("arbitrary", "arbitrary")),
        name="mem_kv",
    )(mem, wkv_bf16)


def _mixer_kernel(x_ref, w_in_ref, b_gate_ref, pool_w_ref, pool_scale_ref, sc_w_ref, cf_w_ref,
                  cf_g_ref, cf_b_ref, gm_g_ref, gm_b_ref, gm_ws_ref, gm_bias_ref, w_br_ref, w_out_ref,
                  ln_g_ref, ln_b_ref, o_ref,
                  pool_e, pool_a, pool_b, pool_c, sc_e, cf_e, cf_y, gates, br_a, br_b, br_c, br_d):
    s = pl.program_id(1)
    tile = x_ref.shape[0]
    end = tile + HALO

    @pl.when(s == 0)
    def _():
        pool_e[0:HALO, :] = jnp.zeros((HALO, MIX_W), F32)
        sc_e[0:SC_HALO, :] = jnp.zeros((SC_HALO, MIX_W), F32)
        cf_e[0:HALO, :] = jnp.zeros((HALO, MIX_W), F32)

    x = x_ref[...]
    xb = x.astype(BF16)

    def proj(lo, hi):
        return _dot(xb, w_in_ref[:, lo:hi])

    lane = lax.broadcasted_iota(jnp.int32, (1, MIX_W), 1)
    lane_group = lane // GROUP_W

    z = proj(0, MIX_W)
    pool_e[HALO:end, :] = z
    pb = proj(COL_SCONV, COL_CONF)
    bg = pb[:, MIX_W:2 * MIX_W]
    u = pb[:, 2 * MIX_W:] * pb[:, :MIX_W]
    sc_e[SC_HALO:SC_HALO + tile, :] = u
    pc = proj(COL_CONF, COL_GMLP)
    cf_e[HALO:end, :] = pc[:, :MIX_W] * jax.nn.sigmoid(pc[:, MIX_W:])
    pd = jax.nn.gelu(proj(COL_GMLP, COL_GATE))
    gu = pd[:, :MIX_W]
    gv = _layer_norm(pd[:, MIX_W:], gm_g_ref[...], gm_b_ref[...]).astype(BF16)
    row = lax.broadcasted_iota(jnp.int32, (CHUNK, CHUNK), 0)
    col = lax.broadcasted_iota(jnp.int32, (CHUNK, CHUNK), 1)
    ws_cat = jnp.concatenate(
        [jnp.where(row >= col, gm_ws_ref[g], 0.0).astype(BF16) for g in range(N_GROUPS)], axis=1)

    def pool_unit():
        pool_a[8:end, :] = pool_e[8:end, :] + pool_e[7:end - 1, :]
        pool_b[16:end, :] = pool_a[16:end, :] + pool_a[14:end - 2, :]
        pool_c[24:end, :] = pool_b[24:end, :] + pool_b[20:end - 4, :]
        s16 = pool_c[HALO:end, :] + pool_c[HALO - 8:end - 8, :]
        ssum = jnp.where(lane_group == 0, pool_a[HALO:end, :],
                         jnp.where(lane_group == 1, pool_b[HALO:end, :],
                                   jnp.where(lane_group == 2, pool_c[HALO:end, :], s16)))
        win = jnp.where(lane_group == 0, 2.0,
                        jnp.where(lane_group == 1, 4.0, jnp.where(lane_group == 2, 8.0, 16.0))).astype(F32)
        count = (s * tile + lax.broadcasted_iota(jnp.int32, (tile, 1), 0) + 1).astype(F32)
        a = ssum / jnp.minimum(count, win) - z
        pool_e[0:HALO, :] = pool_e[tile:end, :]
        br_a[...] = (_dot(a.astype(BF16), pool_w_ref[...]) * pool_scale_ref[...]).astype(BF16)

    def sconv_unit():
        conv = (sc_w_ref[0:1, :] * sc_e[SC_HALO - 2:SC_HALO - 2 + tile, :]
                + sc_w_ref[1:2, :] * sc_e[SC_HALO - 1:SC_HALO - 1 + tile, :]
                + sc_w_ref[2:3, :] * u)
        br_b[...] = (bg * conv).astype(BF16)
        sc_e[0:SC_HALO, :] = sc_e[tile:tile + SC_HALO, :]

    first = HALO - (CONFORMER_CONV_K - 1)

    def conv_phase(phase):
        rows = tile if phase == 0 else tile + V7X_SUBLANES
        acc = None
        for q in range((first + CONFORMER_CONV_K - 1) // V7X_SUBLANES + 1):
            k = q * V7X_SUBLANES + phase - first
            if 0 <= k < CONFORMER_CONV_K:
                term = cf_w_ref[k:k + 1, :] * cf_e[q * V7X_SUBLANES:q * V7X_SUBLANES + rows, :]
                acc = term if acc is None else acc + term
        cf_y[phase, 0:rows, :] = acc

    def conv_finish(part, parts):
        n_rows = tile // parts
        lo = part * n_rows
        acc = cf_y[0, lo:lo + n_rows, :]
        for phase in range(1, V7X_SUBLANES):
            acc = acc + cf_y[phase, lo + phase:lo + phase + n_rows, :]
        cn = _layer_norm(acc, cf_g_ref[...], cf_b_ref[...])
        br_c[lo:lo + n_rows, :] = (cn * jax.nn.sigmoid(cn)).astype(BF16)
        if part == parts - 1:
            cf_e[0:HALO, :] = cf_e[tile:end, :]

    def gmlp_unit(c):
        rows = slice(c * CHUNK, (c + 1) * CHUNK)
        vc = gv[rows, :]
        vblk = jnp.concatenate([jnp.where(lane_group == g, vc, jnp.zeros_like(vc)) for g in range(N_GROUPS)],
                               axis=0)
        sv = _dot(ws_cat, vblk) + gm_bias_ref[...]
        br_d[rows, :] = (gu[rows, :] * sv).astype(BF16)

    n_chunks = tile // CHUNK
    units = ([pool_unit, sconv_unit]
             + [functools.partial(conv_phase, p) for p in range(V7X_SUBLANES)]
             + [functools.partial(gmlp_unit, c) for c in range(n_chunks)]
             + [functools.partial(conv_finish, p, 4) for p in range(4)])
    n_slabs = N_BRANCHES * D_MODEL // GATE_SLAB
    per_slab = -(-len(units) // n_slabs)
    for r in range(n_slabs):
        lo = r * GATE_SLAB
        gates[:, lo:lo + GATE_SLAB] = jax.nn.sigmoid(
            proj(COL_GATE + lo, COL_GATE + lo + GATE_SLAB) + b_gate_ref[:, lo:lo + GATE_SLAB])
        for unit in units[r * per_slab:(r + 1) * per_slab]:
            unit()

    merged = None
    for i, br in enumerate((br_a, br_b, br_c, br_d)):
        term = gates[:, i * D_MODEL:(i + 1) * D_MODEL] * _dot(br[...], w_br_ref[i])
        merged = term if merged is None else merged + term
    mix = _dot(merged.astype(BF16), w_out_ref[...])
    o_ref[...] = _layer_norm(ALPHA * x + mix, ln_g_ref[...], ln_b_ref[...])


def _mixer_call(x, l, w, tile):
    bsz, seq, _ = x.shape
    rows = tile + HALO
    vec = lambda n: _const_spec((None, 1, n), (l, 0, 0))
    in_specs = [
        pl.BlockSpec((None, tile, D_MODEL), lambda b, s: (b, s, 0)),
        _const_spec((None, D_MODEL, IN_COLS), (l, 0, 0)),
        vec(N_BRANCHES * D_MODEL),
        _const_spec((None, MIX_W, MIX_W), (l, 0, 0)),
        vec(MIX_W),
        _const_spec((None, SHORT_CONV_K, MIX_W), (l, 0, 0)),
        _const_spec((None, CONFORMER_CONV_K, MIX_W), (l, 0, 0)),
        vec(MIX_W), vec(MIX_W), vec(MIX_W), vec(MIX_W),
        _const_spec((None, N_GROUPS, CHUNK, CHUNK), (l, 0, 0, 0)),
        _const_spec((None, CHUNK, MIX_W), (l, 0, 0)),
        _const_spec((None, N_BRANCHES, MIX_W, D_MODEL), (l, 0, 0, 0)),
        _const_spec((None, D_MODEL, D_MODEL), (l, 0, 0)),
        vec(D_MODEL), vec(D_MODEL),
    ]
    scratch = [
        pltpu.VMEM((rows, MIX_W), F32), pltpu.VMEM((rows, MIX_W), F32),
        pltpu.VMEM((rows, MIX_W), F32), pltpu.VMEM((rows, MIX_W), F32),
        pltpu.VMEM((tile + 2 * SC_HALO, MIX_W), F32),
        pltpu.VMEM((rows, MIX_W), F32),
        pltpu.VMEM((V7X_SUBLANES, tile + V7X_SUBLANES, MIX_W), F32),
        pltpu.VMEM((tile, N_BRANCHES * D_MODEL), F32),
        pltpu.VMEM((tile, MIX_W), BF16), pltpu.VMEM((tile, MIX_W), BF16),
        pltpu.VMEM((tile, MIX_W), BF16), pltpu.VMEM((tile, MIX_W), BF16),
    ]
    weight_bytes = 2 * (D_MODEL * IN_COLS + N_BRANCHES * MIX_W * D_MODEL + D_MODEL * D_MODEL + MIX_W * MIX_W)
    io_bytes = 2 * 2 * tile * D_MODEL * 4
    work_bytes = 20 * tile * D_MODEL * 4
    return pl.pallas_call(
        _mixer_kernel,
        grid=(bsz, seq // tile),
        in_specs=in_specs,
        out_specs=pl.BlockSpec((None, tile, D_MODEL), lambda b, s: (b, s, 0)),
        out_shape=jax.ShapeDtypeStruct(x.shape, F32),
        scratch_shapes=scratch,
        compiler_params=pltpu.CompilerParams(
            dimension_semantics=("arbitrary", "arbitrary"),
            vmem_limit_bytes=_vmem_limit(weight_bytes + io_bytes + work_bytes)),
        name="mixer",
    )(x, w["w_in"], w["b_gate"], w["pool_w"], w["pool_scale"], w["sc_conv_w"], w["cf_conv_w"],
      w["cf_ln_g"], w["cf_ln_b"], w["gm_ln_g"], w["gm_ln_b"], w["gm_ws"], w["gm_bias"],
      w["w_branch"], w["w_out"], w["ln1_g"], w["ln1_b"])


def _router_scores(x, rw_ref, rb_ref):
    x_hi = x.astype(BF16)
    x_lo = (x - x_hi.astype(F32)).astype(BF16)
    logits = _dot(x_hi, rw_ref[0]) + _dot(x_lo, rw_ref[0]) + _dot(x_hi, rw_ref[1]) + rb_ref[...]
    return jax.nn.sigmoid(logits)


def _expert_slots(scores):
    return [scores if j == 0 else pltpu.roll(scores, V7X_LANES - j * ROUTER_LANE_STRIDE, axis=1)
            for j in range(EXPERTS_PER_GROUP)]


def _top2_weights(sj, keep):
    picked = []
    for j in range(EXPERTS_PER_GROUP):
        rank = jnp.zeros_like(sj[j])
        for i in range(EXPERTS_PER_GROUP):
            if i < j:
                rank = rank + (sj[i] >= sj[j]).astype(F32)
            elif i > j:
                rank = rank + (sj[i] > sj[j]).astype(F32)
        picked.append(jnp.where(keep & (rank < 2.0), sj[j], 0.0))
    denom = jnp.sum(picked[0] + picked[1] + picked[2] + picked[3], axis=-1, keepdims=True)
    return [p / denom for p in picked]


def _xattn_kernel(x_ref, k_ref, v_ref, wq_ref, wo_ref, ln_g_ref, ln_b_ref, rw_ref, rb_ref,
                  o_ref, pos_ref, cnt_ref, base_ref):
    @pl.when((pl.program_id(0) == 0) & (pl.program_id(1) == 0))
    def _():
        base_ref[...] = jnp.zeros_like(base_ref)

    x = x_ref[...]
    tile = x.shape[0]
    q = _dot(x.astype(BF16), wq_ref[...]).astype(BF16)
    heads = []
    for h in range(N_XA_HEADS):
        sl = slice(h * XA_HEAD_DIM, (h + 1) * XA_HEAD_DIM)
        sc = lax.dot_general(q[:, sl], k_ref[:, sl], (((1,), (1,)), ((), ())),
                             preferred_element_type=F32) * (XA_HEAD_DIM ** -0.5)
        sc = sc - jnp.max(sc, axis=-1, keepdims=True)
        e = jnp.exp(sc)
        probs = e / jnp.sum(e, axis=-1, keepdims=True)
        heads.append(_dot(probs.astype(BF16), v_ref[:, sl]).astype(BF16))
    o = jnp.concatenate(heads, axis=-1)
    xa = _dot(o, wo_ref[...])
    x2 = _layer_norm(ALPHA * x + xa, ln_g_ref[...], ln_b_ref[...])
    _store_token_major(o_ref, x2)

    sj = _expert_slots(_router_scores(x2, rw_ref, rb_ref))
    lane = lax.broadcasted_iota(jnp.int32, (1, V7X_LANES), 1)
    hi01, lo01 = jnp.maximum(sj[0], sj[1]), jnp.minimum(sj[0], sj[1])
    hi23, lo23 = jnp.maximum(sj[2], sj[3]), jnp.minimum(sj[2], sj[3])
    top1 = jnp.maximum(hi01, hi23)
    top2 = jnp.maximum(jnp.minimum(hi01, hi23), jnp.maximum(lo01, lo23))
    gscore = jnp.where(lane < N_EXPERT_GROUPS, top1 + top2, -1.0)
    gmax = jnp.max(gscore, axis=-1, keepdims=True)
    sel = jnp.min(jnp.where(gscore == gmax, lane, V7X_LANES), axis=-1, keepdims=True)
    onehot = (lane == sel).astype(F32)

    r = lax.broadcasted_iota(jnp.int32, (tile, tile), 0)
    c = lax.broadcasted_iota(jnp.int32, (tile, tile), 1)
    earlier = _dot((r > c).astype(BF16), onehot.astype(BF16))
    base = base_ref[...]
    rank = jnp.sum(onehot * (earlier + base), axis=-1, keepdims=True)
    base = base + jnp.sum(onehot, axis=0, keepdims=True)
    base_ref[...] = base
    code = sel.astype(F32) * float(1 << RANK_BITS) + rank
    code_rows = jnp.transpose(jnp.broadcast_to(code, (tile, V7X_LANES)))
    pos_ref[...] = code_rows[0:1, :].astype(jnp.int32)
    cnt_ref[...] = jnp.broadcast_to(base, (V7X_SUBLANES, V7X_LANES)).astype(jnp.int32)


def _xattn_call(x, k, v, l, w, tile):
    bsz, seq, _ = x.shape
    steps = seq // tile
    vec = lambda n: _const_spec((None, 1, n), (l, 0, 0))
    weight_bytes = 2 * 2 * D_MODEL * D_MODEL + 2 * 2 * 2 * MEM_LEN * D_MODEL
    io_bytes = 2 * 2 * tile * D_MODEL * 4
    work_bytes = 10 * tile * D_MODEL * 4
    return pl.pallas_call(
        _xattn_kernel,
        grid=(bsz, steps),
        in_specs=[
            pl.BlockSpec((None, tile, D_MODEL), lambda b, s: (b, s, 0)),
            pl.BlockSpec((None, None, MEM_LEN, D_MODEL), lambda b, s: (l, b, 0, 0)),
            pl.BlockSpec((None, None, MEM_LEN, D_MODEL), lambda b, s: (l, b, 0, 0)),
            _const_spec((None, D_MODEL, D_MODEL), (l, 0, 0)),
            _const_spec((None, D_MODEL, D_MODEL), (l, 0, 0)),
            vec(D_MODEL), vec(D_MODEL),
            _const_spec((2, D_MODEL, V7X_LANES), (0, 0, 0)),
            _const_spec((1, V7X_LANES), (0, 0)),
        ],
        out_specs=[
            pl.BlockSpec((tile * CHUNKS, V7X_LANES), lambda b, s: (b * steps + s, 0)),
            pl.BlockSpec((None, 1, tile), lambda b, s: (b * steps + s, 0, 0)),
            pl.BlockSpec((V7X_SUBLANES, V7X_LANES), lambda b, s: (0, 0)),
        ],
        out_shape=[
            jax.ShapeDtypeStruct((bsz * seq * CHUNKS, V7X_LANES), F32),
            jax.ShapeDtypeStruct((bsz * steps, 1, tile), jnp.int32),
            jax.ShapeDtypeStruct((V7X_SUBLANES, V7X_LANES), jnp.int32),
        ],
        scratch_shapes=[pltpu.VMEM((1, V7X_LANES), F32)],
        compiler_params=pltpu.CompilerParams(
            dimension_semantics=("arbitrary", "arbitrary"),
            vmem_limit_bytes=_vmem_limit(weight_bytes + io_bytes + work_bytes)),
        name="xattn",
    )(x, k, v, w["xa_wq"], w["xa_wo"], w["ln2_g"], w["ln2_b"], w["router_w"], w["router_b"])


def _token_rows(ref, t):
    return ref.at[pl.ds(pl.multiple_of(t * CHUNKS, CHUNKS), CHUNKS)]


def _group_tiles(cnt_ref, tile):
    shift = tile.bit_length() - 1
    assert tile == 1 << shift
    firsts, total = [], jnp.int32(0)
    for g in range(N_EXPERT_GROUPS):
        firsts.append(total)
        total = total + lax.shift_right_logical(cnt_ref[g] + (tile - 1), shift)
    return firsts, total


def _sorted_row(code, first_row_ref):
    group = lax.shift_right_logical(code, RANK_BITS)
    return first_row_ref[group] + (code & ((1 << RANK_BITS) - 1))


def _dispatch_kernel(cnt_ref, code_ref, x_ref, xs_ref, zero_ref, first_row_ref, sem):
    tile = x_ref.shape[0] // CHUNKS
    tile_rows = tile * CHUNKS
    n_tiles = xs_ref.shape[0] // tile_rows
    firsts, total = _group_tiles(cnt_ref, tile)
    for g in range(N_EXPERT_GROUPS):
        first_row_ref[g] = firsts[g] * tile

    def clear_tile(t):
        fill = pltpu.make_async_copy(
            zero_ref, xs_ref.at[pl.ds(pl.multiple_of(t * tile_rows, tile_rows), tile_rows)], sem)
        fill.start()
        fill.wait()

    @pl.when(pl.program_id(0) == 0)
    def _():
        zero_ref[...] = jnp.zeros_like(zero_ref)
        ends = firsts[1:] + [total]
        for g in range(N_EXPERT_GROUPS):
            @pl.when(cnt_ref[g] > 0)
            def _():
                clear_tile(ends[g] - 1)
        for t in range(n_tiles - N_EXPERT_GROUPS, n_tiles):
            @pl.when(total <= t)
            def _():
                clear_tile(t)

    def token_copy(r, p):
        return pltpu.make_async_copy(_token_rows(x_ref, r), _token_rows(xs_ref, p), sem)

    def issue(i, carry):
        for j in range(ISSUE_UNROLL):
            r = i * ISSUE_UNROLL + j
            token_copy(r, _sorted_row(code_ref[0, r], first_row_ref)).start(priority=j % 2)
        return carry

    lax.fori_loop(0, tile // ISSUE_UNROLL, issue, 0)
    for _ in range(tile):
        token_copy(0, 0).wait()


def _dispatch_call(x_tm, code, counts, tile):
    n = x_tm.shape[0] // CHUNKS
    n_tiles = n // tile + N_EXPERT_GROUPS
    return pl.pallas_call(
        _dispatch_kernel,
        grid_spec=pltpu.PrefetchScalarGridSpec(
            num_scalar_prefetch=1,
            grid=(n // tile,),
            in_specs=[
                pl.BlockSpec((None, 1, tile), lambda t, cnt: (t, 0, 0), memory_space=pltpu.SMEM),
                pl.BlockSpec((tile * CHUNKS, V7X_LANES), lambda t, cnt: (t, 0)),
            ],
            out_specs=pl.BlockSpec(memory_space=pl.ANY),
            scratch_shapes=[pltpu.VMEM((tile * CHUNKS, V7X_LANES), F32),
                            pltpu.SMEM((N_EXPERT_GROUPS,), jnp.int32),
                            pltpu.SemaphoreType.DMA],
        ),
        out_shape=jax.ShapeDtypeStruct((n_tiles * tile * CHUNKS, V7X_LANES), F32),
        compiler_params=pltpu.CompilerParams(
            dimension_semantics=("arbitrary",),
            vmem_limit_bytes=_vmem_limit(4 * tile * D_MODEL * 4 + (1 << 20))),
        name="dispatch",
    )(counts, code, x_tm)


def _gather_kernel(cnt_ref, code_ref, xs_ref, o_ref, buf_ref, first_row_ref, sem):
    tile = o_ref.shape[0]
    firsts, _ = _group_tiles(cnt_ref, tile)
    for g in range(N_EXPERT_GROUPS):
        first_row_ref[g] = firsts[g] * tile

    def token_copy(r, p):
        return pltpu.make_async_copy(_token_rows(xs_ref, p), _token_rows(buf_ref, r), sem)

    def issue(i, carry):
        for j in range(ISSUE_UNROLL):
            r = i * ISSUE_UNROLL + j
            token_copy(r, _sorted_row(code_ref[0, r], first_row_ref)).start(priority=j % 2)
        return carry

    lax.fori_loop(0, tile // ISSUE_UNROLL, issue, 0)
    for _ in range(tile):
        token_copy(0, 0).wait()
    o_ref[...] = _load_token_major(buf_ref)


def _gather_call(xs, code, counts, n, tile):
    return pl.pallas_call(
        _gather_kernel,
        grid_spec=pltpu.PrefetchScalarGridSpec(
            num_scalar_prefetch=1,
            grid=(n // tile,),
            in_specs=[
                pl.BlockSpec((None, 1, tile), lambda t, cnt: (t, 0, 0), memory_space=pltpu.SMEM),
                pl.BlockSpec(memory_space=pl.ANY),
            ],
            out_specs=pl.BlockSpec((tile, D_MODEL), lambda t, cnt: (t, 0)),
            scratch_shapes=[pltpu.VMEM((tile * CHUNKS, V7X_LANES), F32),
                            pltpu.SMEM((N_EXPERT_GROUPS,), jnp.int32),
                            pltpu.SemaphoreType.DMA],
        ),
        out_shape=jax.ShapeDtypeStruct((n, D_MODEL), F32),
        compiler_params=pltpu.CompilerParams(
            dimension_semantics=("arbitrary",),
            vmem_limit_bytes=_vmem_limit(4 * tile * D_MODEL * 4 + (1 << 20))),
        name="gather",
    )(counts, code, xs)


def _tile_group(step, cnt_ref, tile):
    firsts, total = _group_tiles(cnt_ref, tile)
    group = jnp.int32(0)
    for g in range(1, N_EXPERT_GROUPS):
        group = group + (step >= firsts[g]).astype(jnp.int32)
    return group, step < total


def _moe_kernel(cnt_ref, x_ref, rw_ref, rb_ref, wgu_ref, wdn_ref, ln_g_ref, ln_b_ref, o_ref):
    tile = x_ref.shape[0] // CHUNKS
    _, valid = _tile_group(pl.program_id(0), cnt_ref, tile)

    @pl.when(jnp.logical_not(valid))
    def _():
        o_ref[...] = jnp.zeros_like(o_ref)

    @pl.when(valid)
    def _():
        x = _load_token_major(x_ref)
        xb = x.astype(BF16)
        sj = _expert_slots(_router_scores(x, rw_ref, rb_ref))
        lane = lax.broadcasted_iota(jnp.int32, (1, V7X_LANES), 1)
        comb = _top2_weights(sj, lane == 0)
        hs = []
        for j in range(EXPERTS_PER_GROUP):
            gu = _dot(xb, wgu_ref[j])
            gate, up = gu[:, :D_EXPERT], gu[:, D_EXPERT:]
            hs.append((gate * jax.nn.sigmoid(gate) * up * comb[j][:, 0:1]).astype(BF16))
        y = _dot(jnp.concatenate(hs, axis=-1), wdn_ref[...])
        _store_token_major(o_ref, _layer_norm(ALPHA * x + y, ln_g_ref[...], ln_b_ref[...]))


def _moe_call(xs, counts, l, w, tile):
    hid = EXPERTS_PER_GROUP * D_EXPERT
    n_tiles = xs.shape[0] // (tile * CHUNKS)

    def group_of(t, cnt):
        return _tile_group(t, cnt, tile)[0]

    vec = lambda m: pl.BlockSpec((None, 1, m), lambda t, cnt: (l, 0, 0), pipeline_mode=pl.Buffered(1))
    weight_bytes = 2 * 2 * (D_MODEL * 2 * hid + hid * D_MODEL)
    io_bytes = 2 * 2 * tile * D_MODEL * 4
    work_bytes = 10 * tile * D_MODEL * 4
    return pl.pallas_call(
        _moe_kernel,
        grid_spec=pltpu.PrefetchScalarGridSpec(
            num_scalar_prefetch=1,
            grid=(n_tiles,),
            in_specs=[
                pl.BlockSpec((tile * CHUNKS, V7X_LANES), lambda t, cnt: (t, 0)),
                pl.BlockSpec((None, 2, D_MODEL, V7X_LANES), lambda t, cnt: (group_of(t, cnt), 0, 0, 0)),
                pl.BlockSpec((None, 1, V7X_LANES), lambda t, cnt: (group_of(t, cnt), 0, 0)),
                pl.BlockSpec((None, EXPERTS_PER_GROUP, D_MODEL, 2 * D_EXPERT),
                             lambda t, cnt: (l * N_EXPERT_GROUPS + group_of(t, cnt), 0, 0, 0)),
                pl.BlockSpec((None, hid, D_MODEL),
                             lambda t, cnt: (l * N_EXPERT_GROUPS + group_of(t, cnt), 0, 0)),
                vec(D_MODEL), vec(D_MODEL),
            ],
            out_specs=pl.BlockSpec((tile * CHUNKS, V7X_LANES), lambda t, cnt: (t, 0)),
        ),
        out_shape=jax.ShapeDtypeStruct(xs.shape, F32),
        compiler_params=pltpu.CompilerParams(
            dimension_semantics=("arbitrary",),
            vmem_limit_bytes=_vmem_limit(weight_bytes + io_bytes + work_bytes)),
        name="moe",
    )(counts, xs, w["router_wg"], w["router_bg"], w["moe_wgu"], w["moe_wdn"], w["ln3_g"], w["ln3_b"])


def _prepare(p):
    depth = p["w_in"].shape[0]
    row = lambda a: a.reshape(depth, 1, -1)
    w = {k: p[k].astype(BF16) for k in ("w_in", "w_branch", "w_out", "xa_wq", "xa_wkv", "xa_wo")}
    for k in ("b_gate", "pool_scale", "cf_ln_g", "cf_ln_b", "gm_ln_g", "gm_ln_b",
              "ln1_g", "ln1_b", "ln2_g", "ln2_b", "ln3_g", "ln3_b"):
        w[k] = row(p[k])
    for k in ("sc_conv_w", "cf_conv_w", "gm_ws"):
        w[k] = p[k]
    eye = jnp.eye(N_GROUPS, dtype=F32)
    w["pool_w"] = jnp.einsum("lgcd,gh->lgchd", p["pool_w"], eye).reshape(depth, MIX_W, MIX_W).astype(BF16)
    w["gm_bias"] = jnp.repeat(jnp.swapaxes(p["gm_bs"], 1, 2), GROUP_W, axis=2)
    lanes = (jnp.arange(N_EXPERTS) % EXPERTS_PER_GROUP) * ROUTER_LANE_STRIDE + jnp.arange(N_EXPERTS) // EXPERTS_PER_GROUP
    rw = jnp.zeros((D_MODEL, V7X_LANES), F32).at[:, lanes].set(p["router_w"].astype(F32))
    rw_hi = rw.astype(BF16)
    rw_lo = (rw - rw_hi.astype(F32)).astype(BF16)
    w["router_w"] = jnp.stack([rw_hi, rw_lo])
    w["router_b"] = jnp.zeros((1, V7X_LANES), F32).at[0, lanes].set(p["router_b"].astype(F32))
    slot = jnp.arange(EXPERTS_PER_GROUP) * ROUTER_LANE_STRIDE
    rwg = jnp.zeros((N_EXPERT_GROUPS, D_MODEL, V7X_LANES), F32).at[:, :, slot].set(
        jnp.transpose(p["router_w"].astype(F32).reshape(D_MODEL, N_EXPERT_GROUPS, EXPERTS_PER_GROUP), (1, 0, 2)))
    rwg_hi = rwg.astype(BF16)
    w["router_wg"] = jnp.stack([rwg_hi, (rwg - rwg_hi.astype(F32)).astype(BF16)], axis=1)
    w["router_bg"] = jnp.zeros((N_EXPERT_GROUPS, 1, V7X_LANES), F32).at[:, 0, slot].set(
        p["router_b"].astype(F32).reshape(N_EXPERT_GROUPS, EXPERTS_PER_GROUP))
    w["moe_wgu"] = p["moe_w_gu"].astype(BF16).reshape(
        depth * N_EXPERT_GROUPS, EXPERTS_PER_GROUP, D_MODEL, 2 * D_EXPERT)
    w["moe_wdn"] = p["moe_w_down"].astype(BF16).reshape(
        depth * N_EXPERT_GROUPS, EXPERTS_PER_GROUP * D_EXPERT, D_MODEL)
    return w


def kernel(x, mem, w_in, b_gate, pool_w, pool_scale, sc_conv_w, cf_conv_w, cf_ln_g, cf_ln_b, gm_ln_g, gm_ln_b, gm_ws, gm_bs, w_branch, w_out, ln1_g, ln1_b, xa_wq, xa_wkv, xa_wo, ln2_g, ln2_b, router_w, router_b, moe_w_gu, moe_w_down, ln3_g, ln3_b):
    params = dict(w_in=w_in, b_gate=b_gate, pool_w=pool_w, pool_scale=pool_scale, sc_conv_w=sc_conv_w,
                  cf_conv_w=cf_conv_w, cf_ln_g=cf_ln_g, cf_ln_b=cf_ln_b, gm_ln_g=gm_ln_g, gm_ln_b=gm_ln_b,
                  gm_ws=gm_ws, gm_bs=gm_bs, w_branch=w_branch, w_out=w_out, ln1_g=ln1_g, ln1_b=ln1_b,
                  xa_wq=xa_wq, xa_wkv=xa_wkv, xa_wo=xa_wo, ln2_g=ln2_g, ln2_b=ln2_b, router_w=router_w,
                  router_b=router_b, moe_w_gu=moe_w_gu, moe_w_down=moe_w_down, ln3_g=ln3_g, ln3_b=ln3_b)
    w = _prepare(params)
    bsz, seq, d = x.shape
    n = bsz * seq
    k_all, v_all = _kv_call(mem, w["xa_wkv"])
    tile = 512
    assert n <= 1 << RANK_BITS
    for l in range(DEPTH):
        x = _mixer_call(x, l, w, tile)
        x2, code, cnt = _xattn_call(x, k_all, v_all, l, w, tile)
        counts = cnt[0, :N_EXPERT_GROUPS]
        xs = _dispatch_call(x2, code, counts, tile)
        xs = _moe_call(xs, counts, l, w, tile)
        x = _gather_call(xs, code, counts, n, tile).reshape(bsz, seq, d)
    return x
```

```python
import functools

import jax
import jax.numpy as jnp
from jax import lax
from jax.experimental import pallas as pl
from jax.experimental.pallas import tpu as pltpu

F32 = jnp.float32
BF16 = jnp.bfloat16

D_MODEL = 1024
DEPTH = 4
MIX_W = 256
N_GROUPS = 4
GROUP_W = 64
SHORT_CONV_K = 3
CONFORMER_CONV_K = 31
CHUNK = 128
COL_SCONV = MIX_W
COL_CONF = 4 * MIX_W
COL_GMLP = 6 * MIX_W
COL_GATE = 8 * MIX_W
N_BRANCHES = 4
IN_COLS = COL_GATE + N_BRANCHES * D_MODEL
MEM_LEN = 256
N_XA_HEADS = 4
XA_HEAD_DIM = D_MODEL // N_XA_HEADS
N_EXPERTS = 16
N_EXPERT_GROUPS = 4
EXPERTS_PER_GROUP = 4
D_EXPERT = 256
ALPHA = (2 * DEPTH) ** 0.25
LN_EPS = 1e-5

V7X_SUBLANES = 8
V7X_LANES = 128
V7X_VMEM_BYTES = 64 * 1024 * 1024

HALO = 32
SC_HALO = V7X_SUBLANES
GATE_SLAB = 512
ROUTER_LANE_STRIDE = 32
RANK_BITS = 16
ISSUE_UNROLL = 8
MOE_SUBTILES = 2
XATTN_SUBTILES = 2


def _layer_norm(y, g, b):
    mu = jnp.mean(y, axis=-1, keepdims=True)
    d = y - mu
    var = jnp.mean(d * d, axis=-1, keepdims=True)
    return d * lax.rsqrt(var + LN_EPS) * g + b


def _dot(a, b):
    return jnp.dot(a, b, preferred_element_type=F32)


CHUNKS = D_MODEL // V7X_LANES


def _store_token_major(ref, value, first=0):
    count = value.shape[0]
    for k in range(CHUNKS):
        ref[pl.ds(first * CHUNKS + k, count, stride=CHUNKS), :] = value[:, k * V7X_LANES:(k + 1) * V7X_LANES]


def _load_token_major(ref, first=0, count=None):
    if count is None:
        count = ref.shape[0] // CHUNKS - first
    return jnp.concatenate(
        [ref[pl.ds(first * CHUNKS + k, count, stride=CHUNKS), :] for k in range(CHUNKS)], axis=1)


def _const_spec(shape, index):
    return pl.BlockSpec(shape, lambda *_: index, pipeline_mode=pl.Buffered(1))


def _vmem_limit(nbytes):
    return int(min(V7X_VMEM_BYTES - 4 * 1024 * 1024, nbytes))


def _kv_kernel(mem_ref, wkv_ref, k_ref, v_ref):
    kv = _dot(mem_ref[...].astype(BF16), wkv_ref[...])
    k_ref[...] = kv[:, :D_MODEL].astype(BF16)
    v_ref[...] = kv[:, D_MODEL:].astype(BF16)


def _kv_call(mem, wkv_bf16):
    bsz = mem.shape[0]
    out = jax.ShapeDtypeStruct((DEPTH, bsz, MEM_LEN, D_MODEL), BF16)
    return pl.pallas_call(
        _kv_kernel,
        grid=(DEPTH, bsz),
        in_specs=[
            pl.BlockSpec((None, MEM_LEN, D_MODEL), lambda l, b: (b, 0, 0)),
            pl.BlockSpec((None, D_MODEL, 2 * D_MODEL), lambda l, b: (l, 0, 0)),
        ],
        out_specs=[
            pl.BlockSpec((None, None, MEM_LEN, D_MODEL), lambda l, b: (l, b, 0, 0)),
            pl.BlockSpec((None, None, MEM_LEN, D_MODEL), lambda l, b: (l, b, 0, 0)),
        ],
        out_shape=[out, out],
        compiler_params=pltpu.CompilerParams(dimension_semantics=("arbitrary", "arbitrary")),
        name="mem_kv",
    )(mem, wkv_bf16)


def _prefetched_tokens(cnt_ref, code_first_ref, code_next_ref, xs_ref, buf_ref, first_row_ref, sems, tile):
    step = pl.program_id(0) * pl.num_programs(1) + pl.program_id(1)
    last = pl.num_programs(0) * pl.num_programs(1) - 1
    slot = lax.rem(step, 2)
    firsts, _ = _group_tiles(cnt_ref, tile)
    for g in range(N_EXPERT_GROUPS):
        first_row_ref[g] = firsts[g] * tile

    def token_copy(r, p, to_slot):
        return pltpu.make_async_copy(_token_rows(xs_ref, p), _token_rows(buf_ref.at[to_slot], r), sems.at[to_slot])

    def fetch(code_ref, to_slot):
        def issue(i, carry):
            for j in range(ISSUE_UNROLL):
                r = i * ISSUE_UNROLL + j
                token_copy(r, _sorted_row(code_ref[0, r], first_row_ref), to_slot).start(priority=j % 2)
            return carry

        lax.fori_loop(0, tile // ISSUE_UNROLL, issue, 0)

    @pl.when(step == 0)
    def _():
        fetch(code_first_ref, 0)

    @pl.when(step < last)
    def _():
        fetch(code_next_ref, 1 - slot)

    for _ in range(tile):
        token_copy(0, 0, slot).wait()
    return _load_token_major(buf_ref.at[slot])


def _mixer_kernel(*refs, gathered):
    if gathered:
        cnt_ref, code_first_ref, code_next_ref, xs_ref, *refs = refs
        *refs, tok_buf, first_row_ref, sems = refs
    else:
        x_ref, *refs = refs
    (w_in_ref, b_gate_ref, pool_w_ref, pool_scale_ref, sc_w_ref, cf_w_ref,
     cf_g_ref, cf_b_ref, gm_g_ref, gm_b_ref, gm_ws_ref, gm_bias_ref, w_br_ref, w_out_ref,
     ln_g_ref, ln_b_ref, o_ref,
     pool_e, pool_a, pool_b, pool_c, sc_e, cf_e, cf_y, gates, br_a, br_b, br_c, br_d) = refs
    s = pl.program_id(1)
    tile = o_ref.shape[0]
    end = tile + HALO

    @pl.when(s == 0)
    def _():
        pool_e[0:HALO, :] = jnp.zeros((HALO, MIX_W), F32)
        sc_e[0:SC_HALO, :] = jnp.zeros((SC_HALO, MIX_W), F32)
        cf_e[0:HALO, :] = jnp.zeros((HALO, MIX_W), F32)

    if gathered:
        x = _prefetched_tokens(cnt_ref, code_first_ref, code_next_ref, xs_ref, tok_buf, first_row_ref, sems, tile)
    else:
        x = x_ref[...]
    xb = x.astype(BF16)

    def proj(lo, hi):
        return _dot(xb, w_in_ref[:, lo:hi])

    lane = lax.broadcasted_iota(jnp.int32, (1, MIX_W), 1)
    lane_group = lane // GROUP_W

    z = proj(0, MIX_W)
    pool_e[HALO:end, :] = z
    pb = proj(COL_SCONV, COL_CONF)
    bg = pb[:, MIX_W:2 * MIX_W]
    u = pb[:, 2 * MIX_W:] * pb[:, :MIX_W]
    sc_e[SC_HALO:SC_HALO + tile, :] = u
    pc = proj(COL_CONF, COL_GMLP)
    cf_e[HALO:end, :] = pc[:, :MIX_W] * jax.nn.sigmoid(pc[:, MIX_W:])
    pd = jax.nn.gelu(proj(COL_GMLP, COL_GATE))
    gu = pd[:, :MIX_W]
    gv = _layer_norm(pd[:, MIX_W:], gm_g_ref[...], gm_b_ref[...]).astype(BF16)
    row = lax.broadcasted_iota(jnp.int32, (CHUNK, CHUNK), 0)
    col = lax.broadcasted_iota(jnp.int32, (CHUNK, CHUNK), 1)
    ws_cat = jnp.concatenate(
        [jnp.where(row >= col, gm_ws_ref[g], 0.0).astype(BF16) for g in range(N_GROUPS)], axis=1)

    def pool_unit():
        pool_a[8:end, :] = pool_e[8:end, :] + pool_e[7:end - 1, :]
        pool_b[16:end, :] = pool_a[16:end, :] + pool_a[14:end - 2, :]
        pool_c[24:end, :] = pool_b[24:end, :] + pool_b[20:end - 4, :]
        s16 = pool_c[HALO:end, :] + pool_c[HALO - 8:end - 8, :]
        ssum = jnp.where(lane_group == 0, pool_a[HALO:end, :],
                         jnp.where(lane_group == 1, pool_b[HALO:end, :],
                                   jnp.where(lane_group == 2, pool_c[HALO:end, :], s16)))
        win = jnp.where(lane_group == 0, 2.0,
                        jnp.where(lane_group == 1, 4.0, jnp.where(lane_group == 2, 8.0, 16.0))).astype(F32)
        count = (s * tile + lax.broadcasted_iota(jnp.int32, (tile, 1), 0) + 1).astype(F32)
        a = ssum / jnp.minimum(count, win) - z
        pool_e[0:HALO, :] = pool_e[tile:end, :]
        br_a[...] = (_dot(a.astype(BF16), pool_w_ref[...]) * pool_scale_ref[...]).astype(BF16)

    def sconv_unit():
        conv = (sc_w_ref[0:1, :] * sc_e[SC_HALO - 2:SC_HALO - 2 + tile, :]
                + sc_w_ref[1:2, :] * sc_e[SC_HALO - 1:SC_HALO - 1 + tile, :]
                + sc_w_ref[2:3, :] * u)
        br_b[...] = (bg * conv).astype(BF16)
        sc_e[0:SC_HALO, :] = sc_e[tile:tile + SC_HALO, :]

    first = HALO - (CONFORMER_CONV_K - 1)

    def conv_phase(phase):
        rows = tile if phase == 0 else tile + V7X_SUBLANES
        acc = None
        for q in range((first + CONFORMER_CONV_K - 1) // V7X_SUBLANES + 1):
            k = q * V7X_SUBLANES + phase - first
            if 0 <= k < CONFORMER_CONV_K:
                term = cf_w_ref[k:k + 1, :] * cf_e[q * V7X_SUBLANES:q * V7X_SUBLANES + rows, :]
                acc = term if acc is None else acc + term
        cf_y[phase, 0:rows, :] = acc

    def conv_finish(part, parts):
        n_rows = tile // parts
        lo = part * n_rows
        acc = cf_y[0, lo:lo + n_rows, :]
        for phase in range(1, V7X_SUBLANES):
            acc = acc + cf_y[phase, lo + phase:lo + phase + n_rows, :]
        cn = _layer_norm(acc, cf_g_ref[...], cf_b_ref[...])
        br_c[lo:lo + n_rows, :] = (cn * jax.nn.sigmoid(cn)).astype(BF16)
        if part == parts - 1:
            cf_e[0:HALO, :] = cf_e[tile:end, :]

    def gmlp_unit(c):
        rows = slice(c * CHUNK, (c + 1) * CHUNK)
        vc = gv[rows, :]
        vblk = jnp.concatenate([jnp.where(lane_group == g, vc, jnp.zeros_like(vc)) for g in range(N_GROUPS)],
                               axis=0)
        sv = _dot(ws_cat, vblk) + gm_bias_ref[...]
        br_d[rows, :] = (gu[rows, :] * sv).astype(BF16)

    n_chunks = tile // CHUNK
    units = ([pool_unit, sconv_unit]
             + [functools.partial(conv_phase, p) for p in range(V7X_SUBLANES)]
             + [functools.partial(gmlp_unit, c) for c in range(n_chunks)]
             + [functools.partial(conv_finish, p, 4) for p in range(4)])
    n_slabs = N_BRANCHES * D_MODEL // GATE_SLAB
    per_slab = -(-len(units) // n_slabs)
    for r in range(n_slabs):
        lo = r * GATE_SLAB
        gates[:, lo:lo + GATE_SLAB] = jax.nn.sigmoid(
            proj(COL_GATE + lo, COL_GATE + lo + GATE_SLAB) + b_gate_ref[:, lo:lo + GATE_SLAB])
        for unit in units[r * per_slab:(r + 1) * per_slab]:
            unit()

    merged = None
    for i, br in enumerate((br_a, br_b, br_c, br_d)):
        term = gates[:, i * D_MODEL:(i + 1) * D_MODEL] * _dot(br[...], w_br_ref[i])
        merged = term if merged is None else merged + term
    mix = _dot(merged.astype(BF16), w_out_ref[...])
    o_ref[...] = _layer_norm(ALPHA * x + mix, ln_g_ref[...], ln_b_ref[...])


def _mixer_call(x, l, w, tile, sorted_input=None):
    gathered = sorted_input is not None
    if gathered:
        code, counts, (bsz, seq) = sorted_input
    else:
        bsz, seq, _ = x.shape
    steps = seq // tile
    rows = tile + HALO
    vec = lambda n: _const_spec((None, 1, n), (l, 0, 0))
    in_specs = [
        _const_spec((None, D_MODEL, IN_COLS), (l, 0, 0)),
        vec(N_BRANCHES * D_MODEL),
        _const_spec((None, MIX_W, MIX_W), (l, 0, 0)),
        vec(MIX_W),
        _const_spec((None, SHORT_CONV_K, MIX_W), (l, 0, 0)),
        _const_spec((None, CONFORMER_CONV_K, MIX_W), (l, 0, 0)),
        vec(MIX_W), vec(MIX_W), vec(MIX_W), vec(MIX_W),
        _const_spec((None, N_GROUPS, CHUNK, CHUNK), (l, 0, 0, 0)),
        _const_spec((None, CHUNK, MIX_W), (l, 0, 0)),
        _const_spec((None, N_BRANCHES, MIX_W, D_MODEL), (l, 0, 0, 0)),
        _const_spec((None, D_MODEL, D_MODEL), (l, 0, 0)),
        vec(D_MODEL), vec(D_MODEL),
    ]
    scratch = [
        pltpu.VMEM((rows, MIX_W), F32), pltpu.VMEM((rows, MIX_W), F32),
        pltpu.VMEM((rows, MIX_W), F32), pltpu.VMEM((rows, MIX_W), F32),
        pltpu.VMEM((tile + 2 * SC_HALO, MIX_W), F32),
        pltpu.VMEM((rows, MIX_W), F32),
        pltpu.VMEM((V7X_SUBLANES, tile + V7X_SUBLANES, MIX_W), F32),
        pltpu.VMEM((tile, N_BRANCHES * D_MODEL), F32),
        pltpu.VMEM((tile, MIX_W), BF16), pltpu.VMEM((tile, MIX_W), BF16),
        pltpu.VMEM((tile, MIX_W), BF16), pltpu.VMEM((tile, MIX_W), BF16),
    ]
    weights = (w["w_in"], w["b_gate"], w["pool_w"], w["pool_scale"], w["sc_conv_w"], w["cf_conv_w"],
               w["cf_ln_g"], w["cf_ln_b"], w["gm_ln_g"], w["gm_ln_b"], w["gm_ws"], w["gm_bias"],
               w["w_branch"], w["w_out"], w["ln1_g"], w["ln1_b"])
    if gathered:
        last = bsz * steps - 1
        in_specs = [
            pl.BlockSpec((None, 1, tile), lambda b, s, cnt: (0, 0, 0), memory_space=pltpu.SMEM),
            pl.BlockSpec((None, 1, tile), lambda b, s, cnt: (jnp.minimum(b * steps + s + 1, last), 0, 0),
                         memory_space=pltpu.SMEM),
            pl.BlockSpec(memory_space=pl.ANY),
        ] + in_specs
        scratch = scratch + [pltpu.VMEM((2, tile * CHUNKS, V7X_LANES), F32),
                             pltpu.SMEM((N_EXPERT_GROUPS,), jnp.int32),
                             pltpu.SemaphoreType.DMA((2,))]
        operands = (counts, code, code, x) + weights
    else:
        in_specs = [pl.BlockSpec((None, tile, D_MODEL), lambda b, s: (b, s, 0))] + in_specs
        operands = (x,) + weights
    weight_bytes = 2 * (D_MODEL * IN_COLS + N_BRANCHES * MIX_W * D_MODEL + D_MODEL * D_MODEL + MIX_W * MIX_W)
    io_bytes = 2 * 2 * tile * D_MODEL * 4
    work_bytes = 20 * tile * D_MODEL * 4
    return pl.pallas_call(
        functools.partial(_mixer_kernel, gathered=gathered),
        grid_spec=pltpu.PrefetchScalarGridSpec(
            num_scalar_prefetch=1 if gathered else 0,
            grid=(bsz, steps),
            in_specs=in_specs,
            out_specs=pl.BlockSpec((None, tile, D_MODEL), lambda b, s, *_: (b, s, 0)),
            scratch_shapes=scratch,
        ),
        out_shape=jax.ShapeDtypeStruct((bsz, seq, D_MODEL), F32),
        compiler_params=pltpu.CompilerParams(
            dimension_semantics=("arbitrary", "arbitrary"),
            vmem_limit_bytes=_vmem_limit(weight_bytes + io_bytes + work_bytes)),
        name="mixer",
    )(*operands)


def _router_scores(x, rw_ref, rb_ref):
    x_hi = x.astype(BF16)
    x_lo = (x - x_hi.astype(F32)).astype(BF16)
    logits = _dot(x_hi, rw_ref[0]) + _dot(x_lo, rw_ref[0]) + _dot(x_hi, rw_ref[1]) + rb_ref[...]
    return jax.nn.sigmoid(logits)


def _expert_slots(scores):
    return [scores if j == 0 else pltpu.roll(scores, V7X_LANES - j * ROUTER_LANE_STRIDE, axis=1)
            for j in range(EXPERTS_PER_GROUP)]


def _top2_weights(sj, keep):
    picked = []
    for j in range(EXPERTS_PER_GROUP):
        rank = jnp.zeros_like(sj[j])
        for i in range(EXPERTS_PER_GROUP):
            if i < j:
                rank = rank + (sj[i] >= sj[j]).astype(F32)
            elif i > j:
                rank = rank + (sj[i] > sj[j]).astype(F32)
        picked.append(jnp.where(keep & (rank < 2.0), sj[j], 0.0))
    denom = jnp.sum(picked[0] + picked[1] + picked[2] + picked[3], axis=-1, keepdims=True)
    return [p / denom for p in picked]


def _xattn_kernel(x_ref, k_ref, v_ref, wq_ref, wo_ref, ln_g_ref, ln_b_ref, rw_ref, rb_ref,
                  o_ref, pos_ref, cnt_ref, base_ref):
    @pl.when((pl.program_id(0) == 0) & (pl.program_id(1) == 0))
    def _():
        base_ref[...] = jnp.zeros_like(base_ref)

    tile = x_ref.shape[0]
    sub = tile // XATTN_SUBTILES
    lane = lax.broadcasted_iota(jnp.int32, (1, V7X_LANES), 1)
    r = lax.broadcasted_iota(jnp.int32, (sub, sub), 0)
    c = lax.broadcasted_iota(jnp.int32, (sub, sub), 1)
    before = (r > c).astype(BF16)
    head = lambda h: slice(h * XA_HEAD_DIM, (h + 1) * XA_HEAD_DIM)

    def scores(part):
        x = x_ref[part * sub:(part + 1) * sub, :]
        q = _dot(x.astype(BF16), wq_ref[...]).astype(BF16)
        return x, [lax.dot_general(q[:, head(h)], k_ref[:, head(h)], (((1,), (1,)), ((), ())),
                                   preferred_element_type=F32) * (XA_HEAD_DIM ** -0.5)
                   for h in range(N_XA_HEADS)]

    def softmax(sc):
        e = jnp.exp(sc - jnp.max(sc, axis=-1, keepdims=True))
        return (e / jnp.sum(e, axis=-1, keepdims=True)).astype(BF16)

    def attend(part, x, probs):
        o = jnp.concatenate([_dot(probs[h], v_ref[:, head(h)]).astype(BF16) for h in range(N_XA_HEADS)], axis=-1)
        x2 = _layer_norm(ALPHA * x + _dot(o, wo_ref[...]), ln_g_ref[...], ln_b_ref[...])
        _store_token_major(o_ref, x2, part * sub)
        return _router_scores(x2, rw_ref, rb_ref)

    def select_group(scores_):
        sj = _expert_slots(scores_)
        hi01, lo01 = jnp.maximum(sj[0], sj[1]), jnp.minimum(sj[0], sj[1])
        hi23, lo23 = jnp.maximum(sj[2], sj[3]), jnp.minimum(sj[2], sj[3])
        top1 = jnp.maximum(hi01, hi23)
        top2 = jnp.maximum(jnp.minimum(hi01, hi23), jnp.maximum(lo01, lo23))
        gscore = jnp.where(lane < N_EXPERT_GROUPS, top1 + top2, -1.0)
        gmax = jnp.max(gscore, axis=-1, keepdims=True)
        sel = jnp.min(jnp.where(gscore == gmax, lane, V7X_LANES), axis=-1, keepdims=True)
        return sel, (lane == sel).astype(F32)

    def place(part, sel, onehot, base):
        earlier = _dot(before, onehot.astype(BF16))
        rank = jnp.sum(onehot * (earlier + base), axis=-1, keepdims=True)
        code = sel.astype(F32) * float(1 << RANK_BITS) + rank
        code_rows = jnp.transpose(jnp.broadcast_to(code, (sub, V7X_LANES)))
        pos_ref[:, part * sub:(part + 1) * sub] = code_rows[0:1, :].astype(jnp.int32)
        return base + jnp.sum(onehot, axis=0, keepdims=True)

    base = base_ref[...]
    stage1, stage2, stage3 = {}, {}, {}
    for t in range(XATTN_SUBTILES + 3):
        if t < XATTN_SUBTILES:
            stage1[t] = scores(t)
        if 0 <= t - 1 < XATTN_SUBTILES:
            x, sc = stage1.pop(t - 1)
            stage2[t - 1] = attend(t - 1, x, [softmax(s) for s in sc])
        if 0 <= t - 2 < XATTN_SUBTILES:
            stage3[t - 2] = select_group(stage2.pop(t - 2))
        if 0 <= t - 3 < XATTN_SUBTILES:
            base = place(t - 3, *stage3.pop(t - 3), base)
    base_ref[...] = base
    cnt_ref[...] = jnp.broadcast_to(base, (V7X_SUBLANES, V7X_LANES)).astype(jnp.int32)


def _xattn_call(x, k, v, l, w, tile):
    bsz, seq, _ = x.shape
    steps = seq // tile
    vec = lambda n: _const_spec((None, 1, n), (l, 0, 0))
    weight_bytes = 2 * 2 * D_MODEL * D_MODEL + 2 * 2 * 2 * MEM_LEN * D_MODEL
    io_bytes = 2 * 2 * tile * D_MODEL * 4
    work_bytes = 10 * tile * D_MODEL * 4
    return pl.pallas_call(
        _xattn_kernel,
        grid=(bsz, steps),
        in_specs=[
            pl.BlockSpec((None, tile, D_MODEL), lambda b, s: (b, s, 0)),
            pl.BlockSpec((None, None, MEM_LEN, D_MODEL), lambda b, s: (l, b, 0, 0)),
            pl.BlockSpec((None, None, MEM_LEN, D_MODEL), lambda b, s: (l, b, 0, 0)),
            _const_spec((None, D_MODEL, D_MODEL), (l, 0, 0)),
            _const_spec((None, D_MODEL, D_MODEL), (l, 0, 0)),
            vec(D_MODEL), vec(D_MODEL),
            _const_spec((2, D_MODEL, V7X_LANES), (0, 0, 0)),
            _const_spec((1, V7X_LANES), (0, 0)),
        ],
        out_specs=[
            pl.BlockSpec((tile * CHUNKS, V7X_LANES), lambda b, s: (b * steps + s, 0)),
            pl.BlockSpec((None, 1, tile), lambda b, s: (b * steps + s, 0, 0)),
            pl.BlockSpec((V7X_SUBLANES, V7X_LANES), lambda b, s: (0, 0)),
        ],
        out_shape=[
            jax.ShapeDtypeStruct((bsz * seq * CHUNKS, V7X_LANES), F32),
            jax.ShapeDtypeStruct((bsz * steps, 1, tile), jnp.int32),
            jax.ShapeDtypeStruct((V7X_SUBLANES, V7X_LANES), jnp.int32),
        ],
        scratch_shapes=[pltpu.VMEM((1, V7X_LANES), F32)],
        compiler_params=pltpu.CompilerParams(
            dimension_semantics=("arbitrary", "arbitrary"),
            vmem_limit_bytes=_vmem_limit(weight_bytes + io_bytes + work_bytes)),
        name="xattn",
    )(x, k, v, w["xa_wq"], w["xa_wo"], w["ln2_g"], w["ln2_b"], w["router_w"], w["router_b"])


def _token_rows(ref, t):
    return ref.at[pl.ds(pl.multiple_of(t * CHUNKS, CHUNKS), CHUNKS)]


def _group_tiles(cnt_ref, tile):
    shift = tile.bit_length() - 1
    assert tile == 1 << shift
    firsts, total = [], jnp.int32(0)
    for g in range(N_EXPERT_GROUPS):
        firsts.append(total)
        total = total + lax.shift_right_logical(cnt_ref[g] + (tile - 1), shift)
    return firsts, total


def _sorted_row(code, first_row_ref):
    group = lax.shift_right_logical(code, RANK_BITS)
    return first_row_ref[group] + (code & ((1 << RANK_BITS) - 1))


def _dispatch_kernel(cnt_ref, code_ref, x_ref, xs_ref, zero_ref, first_row_ref, sem):
    tile = x_ref.shape[0] // CHUNKS
    tile_rows = tile * CHUNKS
    n_tiles = xs_ref.shape[0] // tile_rows
    firsts, total = _group_tiles(cnt_ref, tile)
    for g in range(N_EXPERT_GROUPS):
        first_row_ref[g] = firsts[g] * tile

    def clear_tile(t):
        fill = pltpu.make_async_copy(
            zero_ref, xs_ref.at[pl.ds(pl.multiple_of(t * tile_rows, tile_rows), tile_rows)], sem)
        fill.start()
        fill.wait()

    @pl.when(pl.program_id(0) == 0)
    def _():
        zero_ref[...] = jnp.zeros_like(zero_ref)
        ends = firsts[1:] + [total]
        for g in range(N_EXPERT_GROUPS):
            @pl.when(cnt_ref[g] > 0)
            def _():
                clear_tile(ends[g] - 1)
        for t in range(n_tiles - N_EXPERT_GROUPS, n_tiles):
            @pl.when(total <= t)
            def _():
                clear_tile(t)

    def token_copy(r, p):
        return pltpu.make_async_copy(_token_rows(x_ref, r), _token_rows(xs_ref, p), sem)

    def issue(i, carry):
        for j in range(ISSUE_UNROLL):
            r = i * ISSUE_UNROLL + j
            token_copy(r, _sorted_row(code_ref[0, r], first_row_ref)).start(priority=j % 2)
        return carry

    lax.fori_loop(0, tile // ISSUE_UNROLL, issue, 0)
    for _ in range(tile):
        token_copy(0, 0).wait()


def _dispatch_call(x_tm, code, counts, tile):
    n = x_tm.shape[0] // CHUNKS
    n_tiles = n // tile + N_EXPERT_GROUPS
    return pl.pallas_call(
        _dispatch_kernel,
        grid_spec=pltpu.PrefetchScalarGridSpec(
            num_scalar_prefetch=1,
            grid=(n // tile,),
            in_specs=[
                pl.BlockSpec((None, 1, tile), lambda t, cnt: (t, 0, 0), memory_space=pltpu.SMEM),
                pl.BlockSpec((tile * CHUNKS, V7X_LANES), lambda t, cnt: (t, 0)),
            ],
            out_specs=pl.BlockSpec(memory_space=pl.ANY),
            scratch_shapes=[pltpu.VMEM((tile * CHUNKS, V7X_LANES), F32),
                            pltpu.SMEM((N_EXPERT_GROUPS,), jnp.int32),
                            pltpu.SemaphoreType.DMA],
        ),
        out_shape=jax.ShapeDtypeStruct((n_tiles * tile * CHUNKS, V7X_LANES), F32),
        compiler_params=pltpu.CompilerParams(
            dimension_semantics=("arbitrary",),
            vmem_limit_bytes=_vmem_limit(4 * tile * D_MODEL * 4 + (1 << 20))),
        name="dispatch",
    )(counts, code, x_tm)


def _gather_kernel(cnt_ref, code_ref, xs_ref, o_ref, buf_ref, first_row_ref, sem):
    tile = o_ref.shape[0]
    firsts, _ = _group_tiles(cnt_ref, tile)
    for g in range(N_EXPERT_GROUPS):
        first_row_ref[g] = firsts[g] * tile

    def token_copy(r, p):
        return pltpu.make_async_copy(_token_rows(xs_ref, p), _token_rows(buf_ref, r), sem)

    def issue(i, carry):
        for j in range(ISSUE_UNROLL):
            r = i * ISSUE_UNROLL + j
            token_copy(r, _sorted_row(code_ref[0, r], first_row_ref)).start(priority=j % 2)
        return carry

    lax.fori_loop(0, tile // ISSUE_UNROLL, issue, 0)
    for _ in range(tile):
        token_copy(0, 0).wait()
    o_ref[...] = _load_token_major(buf_ref)


def _gather_call(xs, code, counts, n, tile):
    return pl.pallas_call(
        _gather_kernel,
        grid_spec=pltpu.PrefetchScalarGridSpec(
            num_scalar_prefetch=1,
            grid=(n // tile,),
            in_specs=[
                pl.BlockSpec((None, 1, tile), lambda t, cnt: (t, 0, 0), memory_space=pltpu.SMEM),
                pl.BlockSpec(memory_space=pl.ANY),
            ],
            out_specs=pl.BlockSpec((tile, D_MODEL), lambda t, cnt: (t, 0)),
            scratch_shapes=[pltpu.VMEM((tile * CHUNKS, V7X_LANES), F32),
                            pltpu.SMEM((N_EXPERT_GROUPS,), jnp.int32),
                            pltpu.SemaphoreType.DMA],
        ),
        out_shape=jax.ShapeDtypeStruct((n, D_MODEL), F32),
        compiler_params=pltpu.CompilerParams(
            dimension_semantics=("arbitrary",),
            vmem_limit_bytes=_vmem_limit(4 * tile * D_MODEL * 4 + (1 << 20))),
        name="gather",
    )(counts, code, xs)


def _tile_group(step, cnt_ref, tile):
    firsts, total = _group_tiles(cnt_ref, tile)
    group = jnp.int32(0)
    for g in range(1, N_EXPERT_GROUPS):
        group = group + (step >= firsts[g]).astype(jnp.int32)
    return group, step < total


def _moe_kernel(cnt_ref, x_ref, rw_ref, rb_ref, wgu_ref, wdn_ref, ln_g_ref, ln_b_ref, o_ref):
    tile = x_ref.shape[0] // CHUNKS
    _, valid = _tile_group(pl.program_id(0), cnt_ref, tile)

    @pl.when(jnp.logical_not(valid))
    def _():
        o_ref[...] = jnp.zeros_like(o_ref)

    @pl.when(valid)
    def _():
        lane = lax.broadcasted_iota(jnp.int32, (1, V7X_LANES), 1)
        sub = tile // MOE_SUBTILES
        for part in range(MOE_SUBTILES):
            x = _load_token_major(x_ref, part * sub, sub)
            xb = x.astype(BF16)
            sj = _expert_slots(_router_scores(x, rw_ref, rb_ref))
            comb = _top2_weights(sj, lane == 0)
            hs = []
            for j in range(EXPERTS_PER_GROUP):
                gu = _dot(xb, wgu_ref[j])
                gate, up = gu[:, :D_EXPERT], gu[:, D_EXPERT:]
                hs.append((gate * jax.nn.sigmoid(gate) * up * comb[j][:, 0:1]).astype(BF16))
            y = _dot(jnp.concatenate(hs, axis=-1), wdn_ref[...])
            _store_token_major(o_ref, _layer_norm(ALPHA * x + y, ln_g_ref[...], ln_b_ref[...]), part * sub)


def _moe_call(xs, counts, l, w, tile):
    hid = EXPERTS_PER_GROUP * D_EXPERT
    n_tiles = xs.shape[0] // (tile * CHUNKS)

    def group_of(t, cnt):
        return _tile_group(t, cnt, tile)[0]

    vec = lambda m: pl.BlockSpec((None, 1, m), lambda t, cnt: (l, 0, 0), pipeline_mode=pl.Buffered(1))
    weight_bytes = 2 * 2 * (D_MODEL * 2 * hid + hid * D_MODEL)
    io_bytes = 2 * 2 * tile * D_MODEL * 4
    work_bytes = 10 * tile * D_MODEL * 4
    return pl.pallas_call(
        _moe_kernel,
        grid_spec=pltpu.PrefetchScalarGridSpec(
            num_scalar_prefetch=1,
            grid=(n_tiles,),
            in_specs=[
                pl.BlockSpec((tile * CHUNKS, V7X_LANES), lambda t, cnt: (t, 0)),
                pl.BlockSpec((None, 2, D_MODEL, V7X_LANES), lambda t, cnt: (group_of(t, cnt), 0, 0, 0)),
                pl.BlockSpec((None, 1, V7X_LANES), lambda t, cnt: (group_of(t, cnt), 0, 0)),
                pl.BlockSpec((None, EXPERTS_PER_GROUP, D_MODEL, 2 * D_EXPERT),
                             lambda t, cnt: (l * N_EXPERT_GROUPS + group_of(t, cnt), 0, 0, 0)),
                pl.BlockSpec((None, hid, D_MODEL),
                             lambda t, cnt: (l * N_EXPERT_GROUPS + group_of(t, cnt), 0, 0)),
                vec(D_MODEL), vec(D_MODEL),
            ],
            out_specs=pl.BlockSpec((tile * CHUNKS, V7X_LANES), lambda t, cnt: (t, 0)),
        ),
        out_shape=jax.ShapeDtypeStruct(xs.shape, F32),
        compiler_params=pltpu.CompilerParams(
            dimension_semantics=("arbitrary",),
            vmem_limit_bytes=_vmem_limit(weight_bytes + io_bytes + work_bytes)),
        name="moe",
    )(counts, xs, w["router_wg"], w["router_bg"], w["moe_wgu"], w["moe_wdn"], w["ln3_g"], w["ln3_b"])


def _prepare(p):
    depth = p["w_in"].shape[0]
    row = lambda a: a.reshape(depth, 1, -1)
    w = {k: p[k].astype(BF16) for k in ("w_in", "w_branch", "w_out", "xa_wq", "xa_wkv", "xa_wo")}
    for k in ("b_gate", "pool_scale", "cf_ln_g", "cf_ln_b", "gm_ln_g", "gm_ln_b",
              "ln1_g", "ln1_b", "ln2_g", "ln2_b", "ln3_g", "ln3_b"):
        w[k] = row(p[k])
    for k in ("sc_conv_w", "cf_conv_w", "gm_ws"):
        w[k] = p[k]
    eye = jnp.eye(N_GROUPS, dtype=F32)
    w["pool_w"] = jnp.einsum("lgcd,gh->lgchd", p["pool_w"], eye).reshape(depth, MIX_W, MIX_W).astype(BF16)
    w["gm_bias"] = jnp.repeat(jnp.swapaxes(p["gm_bs"], 1, 2), GROUP_W, axis=2)
    lanes = (jnp.arange(N_EXPERTS) % EXPERTS_PER_GROUP) * ROUTER_LANE_STRIDE + jnp.arange(N_EXPERTS) // EXPERTS_PER_GROUP
    rw = jnp.zeros((D_MODEL, V7X_LANES), F32).at[:, lanes].set(p["router_w"].astype(F32))
    rw_hi = rw.astype(BF16)
    rw_lo = (rw - rw_hi.astype(F32)).astype(BF16)
    w["router_w"] = jnp.stack([rw_hi, rw_lo])
    w["router_b"] = jnp.zeros((1, V7X_LANES), F32).at[0, lanes].set(p["router_b"].astype(F32))
    slot = jnp.arange(EXPERTS_PER_GROUP) * ROUTER_LANE_STRIDE
    rwg = jnp.zeros((N_EXPERT_GROUPS, D_MODEL, V7X_LANES), F32).at[:, :, slot].set(
        jnp.transpose(p["router_w"].astype(F32).reshape(D_MODEL, N_EXPERT_GROUPS, EXPERTS_PER_GROUP), (1, 0, 2)))
    rwg_hi = rwg.astype(BF16)
    w["router_wg"] = jnp.stack([rwg_hi, (rwg - rwg_hi.astype(F32)).astype(BF16)], axis=1)
    w["router_bg"] = jnp.zeros((N_EXPERT_GROUPS, 1, V7X_LANES), F32).at[:, 0, slot].set(
        p["router_b"].astype(F32).reshape(N_EXPERT_GROUPS, EXPERTS_PER_GROUP))
    w["moe_wgu"] = p["moe_w_gu"].astype(BF16).reshape(
        depth * N_EXPERT_GROUPS, EXPERTS_PER_GROUP, D_MODEL, 2 * D_EXPERT)
    w["moe_wdn"] = p["moe_w_down"].astype(BF16).reshape(
        depth * N_EXPERT_GROUPS, EXPERTS_PER_GROUP * D_EXPERT, D_MODEL)
    return w


def kernel(x, mem, w_in, b_gate, pool_w, pool_scale, sc_conv_w, cf_conv_w, cf_ln_g, cf_ln_b, gm_ln_g, gm_ln_b, gm_ws, gm_bs, w_branch, w_out, ln1_g, ln1_b, xa_wq, xa_wkv, xa_wo, ln2_g, ln2_b, router_w, router_b, moe_w_gu, moe_w_down, ln3_g, ln3_b):
    params = dict(w_in=w_in, b_gate=b_gate, pool_w=pool_w, pool_scale=pool_scale, sc_conv_w=sc_conv_w,
                  cf_conv_w=cf_conv_w, cf_ln_g=cf_ln_g, cf_ln_b=cf_ln_b, gm_ln_g=gm_ln_g, gm_ln_b=gm_ln_b,
                  gm_ws=gm_ws, gm_bs=gm_bs, w_branch=w_branch, w_out=w_out, ln1_g=ln1_g, ln1_b=ln1_b,
                  xa_wq=xa_wq, xa_wkv=xa_wkv, xa_wo=xa_wo, ln2_g=ln2_g, ln2_b=ln2_b, router_w=router_w,
                  router_b=router_b, moe_w_gu=moe_w_gu, moe_w_down=moe_w_down, ln3_g=ln3_g, ln3_b=ln3_b)
    w = _prepare(params)
    bsz, seq, d = x.shape
    n = bsz * seq
    k_all, v_all = _kv_call(mem, w["xa_wkv"])
    tile = 512
    assert n <= 1 << RANK_BITS
    sorted_input = None
    for l in range(DEPTH):
        x = _mixer_call(x, l, w, tile, sorted_input)
        x2, code, cnt = _xattn_call(x, k_all, v_all, l, w, tile)
        counts = cnt[0, :N_EXPERT_GROUPS]
        x = _moe_call(_dispatch_call(x2, code, counts, tile), counts, l, w, tile)
        sorted_input = (code, counts, (bsz, seq))
    return _gather_call(x, code, counts, n, tile).reshape(bsz, seq, d)
```

```python
import functools

import jax
import jax.numpy as jnp
from jax import lax
from jax.experimental import pallas as pl
from jax.experimental.pallas import tpu as pltpu

F32 = jnp.float32
BF16 = jnp.bfloat16

D_MODEL = 1024
DEPTH = 4
MIX_W = 256
N_GROUPS = 4
GROUP_W = 64
SHORT_CONV_K = 3
CONFORMER_CONV_K = 31
CHUNK = 128
COL_SCONV = MIX_W
COL_CONF = 4 * MIX_W
COL_GMLP = 6 * MIX_W
COL_GATE = 8 * MIX_W
N_BRANCHES = 4
IN_COLS = COL_GATE + N_BRANCHES * D_MODEL
MEM_LEN = 256
N_XA_HEADS = 4
XA_HEAD_DIM = D_MODEL // N_XA_HEADS
N_EXPERTS = 16
N_EXPERT_GROUPS = 4
EXPERTS_PER_GROUP = 4
D_EXPERT = 256
ALPHA = (2 * DEPTH) ** 0.25
LN_EPS = 1e-5

V7X_SUBLANES = 8
V7X_LANES = 128
V7X_VMEM_BYTES = 64 * 1024 * 1024

HALO = 32
SC_HALO = V7X_SUBLANES
GATE_SLAB = 512
ROUTER_LANE_STRIDE = 32
RANK_BITS = 16
ISSUE_UNROLL = 8
MOE_SUBTILES = 2
XATTN_SUBTILES = 2


def _layer_norm(y, g, b):
    mu = jnp.mean(y, axis=-1, keepdims=True)
    d = y - mu
    var = jnp.mean(d * d, axis=-1, keepdims=True)
    return d * lax.rsqrt(var + LN_EPS) * g + b


def _dot(a, b):
    return jnp.dot(a, b, preferred_element_type=F32)


CHUNKS = D_MODEL // V7X_LANES


def _store_token_major(ref, value, first=0):
    count = value.shape[0]
    for k in range(CHUNKS):
        ref[pl.ds(first * CHUNKS + k, count, stride=CHUNKS), :] = value[:, k * V7X_LANES:(k + 1) * V7X_LANES]


def _load_token_major(ref, first=0, count=None):
    if count is None:
        count = ref.shape[0] // CHUNKS - first
    return jnp.concatenate(
        [ref[pl.ds(first * CHUNKS + k, count, stride=CHUNKS), :] for k in range(CHUNKS)], axis=1)


def _const_spec(shape, index):
    return pl.BlockSpec(shape, lambda *_: index, pipeline_mode=pl.Buffered(1))


def _vmem_limit(nbytes):
    return int(min(V7X_VMEM_BYTES - 4 * 1024 * 1024, nbytes))


def _kv_kernel(mem_ref, wkv_ref, k_ref, v_ref):
    kv = _dot(mem_ref[...].astype(BF16), wkv_ref[...])
    k_ref[...] = kv[:, :D_MODEL].astype(BF16)
    v_ref[...] = kv[:, D_MODEL:].astype(BF16)


def _kv_call(mem, wkv_bf16):
    bsz = mem.shape[0]
    out = jax.ShapeDtypeStruct((DEPTH, bsz, MEM_LEN, D_MODEL), BF16)
    return pl.pallas_call(
        _kv_kernel,
        grid=(DEPTH, bsz),
        in_specs=[
            pl.BlockSpec((None, MEM_LEN, D_MODEL), lambda l, b: (b, 0, 0)),
            pl.BlockSpec((None, D_MODEL, 2 * D_MODEL), lambda l, b: (l, 0, 0)),
        ],
        out_specs=[
            pl.BlockSpec((None, None, MEM_LEN, D_MODEL), lambda l, b: (l, b, 0, 0)),
            pl.BlockSpec((None, None, MEM_LEN, D_MODEL), lambda l, b: (l, b, 0, 0)),
        ],
        out_shape=[out, out],
        compiler_params=pltpu.CompilerParams(dimension_semantics=("arbitrary", "arbitrary")),
        name="mem_kv",
    )(mem, wkv_bf16)


def _prefetched_tokens(cnt_ref, code_first_ref, code_next_ref, xs_ref, buf_ref, first_row_ref, sems, tile):
    step = pl.program_id(0) * pl.num_programs(1) + pl.program_id(1)
    last = pl.num_programs(0) * pl.num_programs(1) - 1
    slot = lax.rem(step, 2)
    firsts, _ = _group_tiles(cnt_ref, tile)
    for g in range(N_EXPERT_GROUPS):
        first_row_ref[g] = firsts[g] * tile

    def token_copy(r, p, to_slot):
        return pltpu.make_async_copy(_token_rows(xs_ref, p), _token_rows(buf_ref.at[to_slot], r), sems.at[to_slot])

    def fetch(code_ref, to_slot):
        def issue(i, carry):
            for j in range(ISSUE_UNROLL):
                r = i * ISSUE_UNROLL + j
                token_copy(r, _sorted_row(code_ref[0, r], first_row_ref), to_slot).start(priority=j % 2)
            return carry

        lax.fori_loop(0, tile // ISSUE_UNROLL, issue, 0)

    @pl.when(step == 0)
    def _():
        fetch(code_first_ref, 0)

    for _ in range(tile):
        token_copy(0, 0, slot).wait()
    x = _load_token_major(buf_ref.at[slot])

    for r in range(tile):
        token_copy(r, _sorted_row(code_next_ref[0, r], first_row_ref), 1 - slot).start(priority=r % 2)

    def drain_last():
        @pl.when(step == last)
        def _():
            for _ in range(tile):
                token_copy(0, 0, 1 - slot).wait()

    return x, drain_last


def _mixer_kernel(*refs, gathered):
    if gathered:
        cnt_ref, code_first_ref, code_next_ref, xs_ref, *refs = refs
        *refs, tok_buf, first_row_ref, sems = refs
    else:
        x_ref, *refs = refs
    (w_in_ref, b_gate_ref, pool_w_ref, pool_scale_ref, sc_w_ref, cf_w_ref,
     cf_g_ref, cf_b_ref, gm_g_ref, gm_b_ref, gm_ws_ref, gm_bias_ref, w_br_ref, w_out_ref,
     ln_g_ref, ln_b_ref, o_ref,
     pool_e, pool_a, pool_b, pool_c, sc_e, cf_e, cf_y, gates, br_a, br_b, br_c, br_d) = refs
    s = pl.program_id(1)
    tile = o_ref.shape[0]
    end = tile + HALO

    @pl.when(s == 0)
    def _():
        pool_e[0:HALO, :] = jnp.zeros((HALO, MIX_W), F32)
        sc_e[0:SC_HALO, :] = jnp.zeros((SC_HALO, MIX_W), F32)
        cf_e[0:HALO, :] = jnp.zeros((HALO, MIX_W), F32)

    if gathered:
        x, drain_last = _prefetched_tokens(
            cnt_ref, code_first_ref, code_next_ref, xs_ref, tok_buf, first_row_ref, sems, tile)
    else:
        x = x_ref[...]
    xb = x.astype(BF16)

    def proj(lo, hi):
        return _dot(xb, w_in_ref[:, lo:hi])

    lane = lax.broadcasted_iota(jnp.int32, (1, MIX_W), 1)
    lane_group = lane // GROUP_W

    z = proj(0, MIX_W)
    pool_e[HALO:end, :] = z
    pb = proj(COL_SCONV, COL_CONF)
    bg = pb[:, MIX_W:2 * MIX_W]
    u = pb[:, 2 * MIX_W:] * pb[:, :MIX_W]
    sc_e[SC_HALO:SC_HALO + tile, :] = u
    pc = proj(COL_CONF, COL_GMLP)
    cf_e[HALO:end, :] = pc[:, :MIX_W] * jax.nn.sigmoid(pc[:, MIX_W:])
    pd = jax.nn.gelu(proj(COL_GMLP, COL_GATE))
    gu = pd[:, :MIX_W]
    gv = _layer_norm(pd[:, MIX_W:], gm_g_ref[...], gm_b_ref[...]).astype(BF16)
    row = lax.broadcasted_iota(jnp.int32, (CHUNK, CHUNK), 0)
    col = lax.broadcasted_iota(jnp.int32, (CHUNK, CHUNK), 1)
    ws_cat = jnp.concatenate(
        [jnp.where(row >= col, gm_ws_ref[g], 0.0).astype(BF16) for g in range(N_GROUPS)], axis=1)

    def pool_unit():
        pool_a[8:end, :] = pool_e[8:end, :] + pool_e[7:end - 1, :]
        pool_b[16:end, :] = pool_a[16:end, :] + pool_a[14:end - 2, :]
        pool_c[24:end, :] = pool_b[24:end, :] + pool_b[20:end - 4, :]
        s16 = pool_c[HALO:end, :] + pool_c[HALO - 8:end - 8, :]
        ssum = jnp.where(lane_group == 0, pool_a[HALO:end, :],
                         jnp.where(lane_group == 1, pool_b[HALO:end, :],
                                   jnp.where(lane_group == 2, pool_c[HALO:end, :], s16)))
        win = jnp.where(lane_group == 0, 2.0,
                        jnp.where(lane_group == 1, 4.0, jnp.where(lane_group == 2, 8.0, 16.0))).astype(F32)
        count = (s * tile + lax.broadcasted_iota(jnp.int32, (tile, 1), 0) + 1).astype(F32)
        a = ssum / jnp.minimum(count, win) - z
        pool_e[0:HALO, :] = pool_e[tile:end, :]
        br_a[...] = (_dot(a.astype(BF16), pool_w_ref[...]) * pool_scale_ref[...]).astype(BF16)

    def sconv_unit():
        conv = (sc_w_ref[0:1, :] * sc_e[SC_HALO - 2:SC_HALO - 2 + tile, :]
                + sc_w_ref[1:2, :] * sc_e[SC_HALO - 1:SC_HALO - 1 + tile, :]
                + sc_w_ref[2:3, :] * u)
        br_b[...] = (bg * conv).astype(BF16)
        sc_e[0:SC_HALO, :] = sc_e[tile:tile + SC_HALO, :]

    first = HALO - (CONFORMER_CONV_K - 1)

    def conv_phase(phase):
        rows = tile if phase == 0 else tile + V7X_SUBLANES
        acc = None
        for q in range((first + CONFORMER_CONV_K - 1) // V7X_SUBLANES + 1):
            k = q * V7X_SUBLANES + phase - first
            if 0 <= k < CONFORMER_CONV_K:
                term = cf_w_ref[k:k + 1, :] * cf_e[q * V7X_SUBLANES:q * V7X_SUBLANES + rows, :]
                acc = term if acc is None else acc + term
        cf_y[phase, 0:rows, :] = acc

    def conv_finish(part, parts):
        n_rows = tile // parts
        lo = part * n_rows
        acc = cf_y[0, lo:lo + n_rows, :]
        for phase in range(1, V7X_SUBLANES):
            acc = acc + cf_y[phase, lo + phase:lo + phase + n_rows, :]
        cn = _layer_norm(acc, cf_g_ref[...], cf_b_ref[...])
        br_c[lo:lo + n_rows, :] = (cn * jax.nn.sigmoid(cn)).astype(BF16)
        if part == parts - 1:
            cf_e[0:HALO, :] = cf_e[tile:end, :]

    def gmlp_unit(c):
        rows = slice(c * CHUNK, (c + 1) * CHUNK)
        vc = gv[rows, :]
        vblk = jnp.concatenate([jnp.where(lane_group == g, vc, jnp.zeros_like(vc)) for g in range(N_GROUPS)],
                               axis=0)
        sv = _dot(ws_cat, vblk) + gm_bias_ref[...]
        br_d[rows, :] = (gu[rows, :] * sv).astype(BF16)

    n_chunks = tile // CHUNK
    units = ([pool_unit, sconv_unit]
             + [functools.partial(conv_phase, p) for p in range(V7X_SUBLANES)]
             + [functools.partial(gmlp_unit, c) for c in range(n_chunks)]
             + [functools.partial(conv_finish, p, 4) for p in range(4)])
    n_slabs = N_BRANCHES * D_MODEL // GATE_SLAB
    per_slab = -(-len(units) // n_slabs)
    for r in range(n_slabs):
        lo = r * GATE_SLAB
        gates[:, lo:lo + GATE_SLAB] = jax.nn.sigmoid(
            proj(COL_GATE + lo, COL_GATE + lo + GATE_SLAB) + b_gate_ref[:, lo:lo + GATE_SLAB])
        for unit in units[r * per_slab:(r + 1) * per_slab]:
            unit()

    merged = None
    for i, br in enumerate((br_a, br_b, br_c, br_d)):
        term = gates[:, i * D_MODEL:(i + 1) * D_MODEL] * _dot(br[...], w_br_ref[i])
        merged = term if merged is None else merged + term
    mix = _dot(merged.astype(BF16), w_out_ref[...])
    o_ref[...] = _layer_norm(ALPHA * x + mix, ln_g_ref[...], ln_b_ref[...])
    if gathered:
        drain_last()


def _mixer_call(x, l, w, tile, sorted_input=None):
    gathered = sorted_input is not None
    if gathered:
        code, counts, (bsz, seq) = sorted_input
    else:
        bsz, seq, _ = x.shape
    steps = seq // tile
    rows = tile + HALO
    vec = lambda n: _const_spec((None, 1, n), (l, 0, 0))
    in_specs = [
        _const_spec((None, D_MODEL, IN_COLS), (l, 0, 0)),
        vec(N_BRANCHES * D_MODEL),
        _const_spec((None, MIX_W, MIX_W), (l, 0, 0)),
        vec(MIX_W),
        _const_spec((None, SHORT_CONV_K, MIX_W), (l, 0, 0)),
        _const_spec((None, CONFORMER_CONV_K, MIX_W), (l, 0, 0)),
        vec(MIX_W), vec(MIX_W), vec(MIX_W), vec(MIX_W),
        _const_spec((None, N_GROUPS, CHUNK, CHUNK), (l, 0, 0, 0)),
        _const_spec((None, CHUNK, MIX_W), (l, 0, 0)),
        _const_spec((None, N_BRANCHES, MIX_W, D_MODEL), (l, 0, 0, 0)),
        _const_spec((None, D_MODEL, D_MODEL), (l, 0, 0)),
        vec(D_MODEL), vec(D_MODEL),
    ]
    scratch = [
        pltpu.VMEM((rows, MIX_W), F32), pltpu.VMEM((rows, MIX_W), F32),
        pltpu.VMEM((rows, MIX_W), F32), pltpu.VMEM((rows, MIX_W), F32),
        pltpu.VMEM((tile + 2 * SC_HALO, MIX_W), F32),
        pltpu.VMEM((rows, MIX_W), F32),
        pltpu.VMEM((V7X_SUBLANES, tile + V7X_SUBLANES, MIX_W), F32),
        pltpu.VMEM((tile, N_BRANCHES * D_MODEL), F32),
        pltpu.VMEM((tile, MIX_W), BF16), pltpu.VMEM((tile, MIX_W), BF16),
        pltpu.VMEM((tile, MIX_W), BF16), pltpu.VMEM((tile, MIX_W), BF16),
    ]
    weights = (w["w_in"], w["b_gate"], w["pool_w"], w["pool_scale"], w["sc_conv_w"], w["cf_conv_w"],
               w["cf_ln_g"], w["cf_ln_b"], w["gm_ln_g"], w["gm_ln_b"], w["gm_ws"], w["gm_bias"],
               w["w_branch"], w["w_out"], w["ln1_g"], w["ln1_b"])
    if gathered:
        last = bsz * steps - 1
        in_specs = [
            pl.BlockSpec((None, 1, tile), lambda b, s, cnt: (0, 0, 0), memory_space=pltpu.SMEM),
            pl.BlockSpec((None, 1, tile), lambda b, s, cnt: (jnp.minimum(b * steps + s + 1, last), 0, 0),
                         memory_space=pltpu.SMEM),
            pl.BlockSpec(memory_space=pl.ANY),
        ] + in_specs
        scratch = scratch + [pltpu.VMEM((2, tile * CHUNKS, V7X_LANES), F32),
                             pltpu.SMEM((N_EXPERT_GROUPS,), jnp.int32),
                             pltpu.SemaphoreType.DMA((2,))]
        operands = (counts, code, code, x) + weights
    else:
        in_specs = [pl.BlockSpec((None, tile, D_MODEL), lambda b, s: (b, s, 0))] + in_specs
        operands = (x,) + weights
    weight_bytes = 2 * (D_MODEL * IN_COLS + N_BRANCHES * MIX_W * D_MODEL + D_MODEL * D_MODEL + MIX_W * MIX_W)
    io_bytes = 2 * 2 * tile * D_MODEL * 4
    work_bytes = 20 * tile * D_MODEL * 4
    return pl.pallas_call(
        functools.partial(_mixer_kernel, gathered=gathered),
        grid_spec=pltpu.PrefetchScalarGridSpec(
            num_scalar_prefetch=1 if gathered else 0,
            grid=(bsz, steps),
            in_specs=in_specs,
            out_specs=pl.BlockSpec((None, tile, D_MODEL), lambda b, s, *_: (b, s, 0)),
            scratch_shapes=scratch,
        ),
        out_shape=jax.ShapeDtypeStruct((bsz, seq, D_MODEL), F32),
        compiler_params=pltpu.CompilerParams(
            dimension_semantics=("arbitrary", "arbitrary"),
            vmem_limit_bytes=_vmem_limit(weight_bytes + io_bytes + work_bytes)),
        name="mixer",
    )(*operands)


def _router_scores(x, rw_ref, rb_ref):
    x_hi = x.astype(BF16)
    x_lo = (x - x_hi.astype(F32)).astype(BF16)
    by_hi = _dot(x_hi, rw_ref[...])
    by_lo = _dot(x_lo, rw_ref[:, :V7X_LANES])
    return jax.nn.sigmoid(by_hi[:, :V7X_LANES] + by_hi[:, V7X_LANES:] + by_lo + rb_ref[...])


def _expert_slots(scores):
    return [scores if j == 0 else pltpu.roll(scores, V7X_LANES - j * ROUTER_LANE_STRIDE, axis=1)
            for j in range(EXPERTS_PER_GROUP)]


def _top2_weights(sj, keep):
    picked = []
    for j in range(EXPERTS_PER_GROUP):
        rank = jnp.zeros_like(sj[j])
        for i in range(EXPERTS_PER_GROUP):
            if i < j:
                rank = rank + (sj[i] >= sj[j]).astype(F32)
            elif i > j:
                rank = rank + (sj[i] > sj[j]).astype(F32)
        picked.append(jnp.where(keep & (rank < 2.0), sj[j], 0.0))
    denom = jnp.sum(picked[0] + picked[1] + picked[2] + picked[3], axis=-1, keepdims=True)
    return [p / denom for p in picked]


def _xattn_kernel(x_ref, k_ref, v_ref, wq_ref, wo_ref, ln_g_ref, ln_b_ref, rw_ref, rb_ref,
                  o_ref, pos_ref, cnt_ref, base_ref):
    @pl.when((pl.program_id(0) == 0) & (pl.program_id(1) == 0))
    def _():
        base_ref[...] = jnp.zeros_like(base_ref)

    tile = x_ref.shape[0]
    sub = tile // XATTN_SUBTILES
    lane = lax.broadcasted_iota(jnp.int32, (1, V7X_LANES), 1)
    r = lax.broadcasted_iota(jnp.int32, (sub, sub), 0)
    c = lax.broadcasted_iota(jnp.int32, (sub, sub), 1)
    before = (r > c).astype(BF16)
    head = lambda h: slice(h * XA_HEAD_DIM, (h + 1) * XA_HEAD_DIM)

    def scores(part):
        x = x_ref[part * sub:(part + 1) * sub, :]
        q = _dot(x.astype(BF16), wq_ref[...]).astype(BF16)
        return x, [lax.dot_general(q[:, head(h)], k_ref[:, head(h)], (((1,), (1,)), ((), ())),
                                   preferred_element_type=F32) * (XA_HEAD_DIM ** -0.5)
                   for h in range(N_XA_HEADS)]

    def softmax(sc):
        e = jnp.exp(sc - jnp.max(sc, axis=-1, keepdims=True))
        return (e / jnp.sum(e, axis=-1, keepdims=True)).astype(BF16)

    def attend(part, x, probs):
        o = jnp.concatenate([_dot(probs[h], v_ref[:, head(h)]).astype(BF16) for h in range(N_XA_HEADS)], axis=-1)
        x2 = _layer_norm(ALPHA * x + _dot(o, wo_ref[...]), ln_g_ref[...], ln_b_ref[...])
        _store_token_major(o_ref, x2, part * sub)
        return _router_scores(x2, rw_ref, rb_ref)

    def select_group(scores_):
        sj = _expert_slots(scores_)
        hi01, lo01 = jnp.maximum(sj[0], sj[1]), jnp.minimum(sj[0], sj[1])
        hi23, lo23 = jnp.maximum(sj[2], sj[3]), jnp.minimum(sj[2], sj[3])
        top1 = jnp.maximum(hi01, hi23)
        top2 = jnp.maximum(jnp.minimum(hi01, hi23), jnp.maximum(lo01, lo23))
        gscore = jnp.where(lane < N_EXPERT_GROUPS, top1 + top2, -1.0)
        gmax = jnp.max(gscore, axis=-1, keepdims=True)
        sel = jnp.min(jnp.where(gscore == gmax, lane, V7X_LANES), axis=-1, keepdims=True)
        return sel, (lane == sel).astype(F32)

    def place(part, sel, onehot, base):
        earlier = _dot(before, onehot.astype(BF16))
        rank = jnp.sum(onehot * (earlier + base), axis=-1, keepdims=True)
        code = sel.astype(F32) * float(1 << RANK_BITS) + rank
        code_rows = jnp.transpose(jnp.broadcast_to(code, (sub, V7X_LANES)))
        pos_ref[:, part * sub:(part + 1) * sub] = code_rows[0:1, :].astype(jnp.int32)
        return base + jnp.sum(onehot, axis=0, keepdims=True)

    base = base_ref[...]
    stage1, stage2, stage3 = {}, {}, {}
    for t in range(XATTN_SUBTILES + 3):
        if t < XATTN_SUBTILES:
            stage1[t] = scores(t)
        if 0 <= t - 1 < XATTN_SUBTILES:
            x, sc = stage1.pop(t - 1)
            stage2[t - 1] = attend(t - 1, x, [softmax(s) for s in sc])
        if 0 <= t - 2 < XATTN_SUBTILES:
            stage3[t - 2] = select_group(stage2.pop(t - 2))
        if 0 <= t - 3 < XATTN_SUBTILES:
            base = place(t - 3, *stage3.pop(t - 3), base)
    base_ref[...] = base
    cnt_ref[...] = jnp.broadcast_to(base, (V7X_SUBLANES, V7X_LANES)).astype(jnp.int32)


def _xattn_call(x, k, v, l, w, tile):
    bsz, seq, _ = x.shape
    steps = seq // tile
    vec = lambda n: _const_spec((None, 1, n), (l, 0, 0))
    weight_bytes = 2 * 2 * D_MODEL * D_MODEL + 2 * 2 * 2 * MEM_LEN * D_MODEL
    io_bytes = 2 * 2 * tile * D_MODEL * 4
    work_bytes = 10 * tile * D_MODEL * 4
    return pl.pallas_call(
        _xattn_kernel,
        grid=(bsz, steps),
        in_specs=[
            pl.BlockSpec((None, tile, D_MODEL), lambda b, s: (b, s, 0)),
            pl.BlockSpec((None, None, MEM_LEN, D_MODEL), lambda b, s: (l, b, 0, 0)),
            pl.BlockSpec((None, None, MEM_LEN, D_MODEL), lambda b, s: (l, b, 0, 0)),
            _const_spec((None, D_MODEL, D_MODEL), (l, 0, 0)),
            _const_spec((None, D_MODEL, D_MODEL), (l, 0, 0)),
            vec(D_MODEL), vec(D_MODEL),
            _const_spec((D_MODEL, 2 * V7X_LANES), (0, 0)),
            _const_spec((1, V7X_LANES), (0, 0)),
        ],
        out_specs=[
            pl.BlockSpec((tile * CHUNKS, V7X_LANES), lambda b, s: (b * steps + s, 0)),
            pl.BlockSpec((None, 1, tile), lambda b, s: (b * steps + s, 0, 0)),
            pl.BlockSpec((V7X_SUBLANES, V7X_LANES), lambda b, s: (0, 0)),
        ],
        out_shape=[
            jax.ShapeDtypeStruct((bsz * seq * CHUNKS, V7X_LANES), F32),
            jax.ShapeDtypeStruct((bsz * steps, 1, tile), jnp.int32),
            jax.ShapeDtypeStruct((V7X_SUBLANES, V7X_LANES), jnp.int32),
        ],
        scratch_shapes=[pltpu.VMEM((1, V7X_LANES), F32)],
        compiler_params=pltpu.CompilerParams(
            dimension_semantics=("arbitrary", "arbitrary"),
            vmem_limit_bytes=_vmem_limit(weight_bytes + io_bytes + work_bytes)),
        name="xattn",
    )(x, k, v, w["xa_wq"], w["xa_wo"], w["ln2_g"], w["ln2_b"], w["router_w"], w["router_b"])


def _token_rows(ref, t):
    return ref.at[pl.ds(pl.multiple_of(t * CHUNKS, CHUNKS), CHUNKS)]


def _group_tiles(cnt_ref, tile):
    shift = tile.bit_length() - 1
    assert tile == 1 << shift
    firsts, total = [], jnp.int32(0)
    for g in range(N_EXPERT_GROUPS):
        firsts.append(total)
        total = total + lax.shift_right_logical(cnt_ref[g] + (tile - 1), shift)
    return firsts, total


def _sorted_row(code, first_row_ref):
    group = lax.shift_right_logical(code, RANK_BITS)
    return first_row_ref[group] + (code & ((1 << RANK_BITS) - 1))


def _dispatch_kernel(cnt_ref, code_ref, x_ref, xs_ref, zero_ref, first_row_ref, sem):
    tile = x_ref.shape[0] // CHUNKS
    tile_rows = tile * CHUNKS
    n_tiles = xs_ref.shape[0] // tile_rows
    firsts, total = _group_tiles(cnt_ref, tile)
    for g in range(N_EXPERT_GROUPS):
        first_row_ref[g] = firsts[g] * tile

    def clear_tile(t):
        fill = pltpu.make_async_copy(
            zero_ref, xs_ref.at[pl.ds(pl.multiple_of(t * tile_rows, tile_rows), tile_rows)], sem)
        fill.start()
        fill.wait()

    @pl.when(pl.program_id(0) == 0)
    def _():
        zero_ref[...] = jnp.zeros_like(zero_ref)
        ends = firsts[1:] + [total]
        for g in range(N_EXPERT_GROUPS):
            @pl.when(cnt_ref[g] > 0)
            def _():
                clear_tile(ends[g] - 1)
        for t in range(n_tiles - N_EXPERT_GROUPS, n_tiles):
            @pl.when(total <= t)
            def _():
                clear_tile(t)

    def token_copy(r, p):
        return pltpu.make_async_copy(_token_rows(x_ref, r), _token_rows(xs_ref, p), sem)

    def issue(i, carry):
        for j in range(ISSUE_UNROLL):
            r = i * ISSUE_UNROLL + j
            token_copy(r, _sorted_row(code_ref[0, r], first_row_ref)).start(priority=j % 2)
        return carry

    lax.fori_loop(0, tile // ISSUE_UNROLL, issue, 0)
    for _ in range(tile):
        token_copy(0, 0).wait()


def _dispatch_call(x_tm, code, counts, tile):
    n = x_tm.shape[0] // CHUNKS
    n_tiles = n // tile + N_EXPERT_GROUPS
    return pl.pallas_call(
        _dispatch_kernel,
        grid_spec=pltpu.PrefetchScalarGridSpec(
            num_scalar_prefetch=1,
            grid=(n // tile,),
            in_specs=[
                pl.BlockSpec((None, 1, tile), lambda t, cnt: (t, 0, 0), memory_space=pltpu.SMEM),
                pl.BlockSpec((tile * CHUNKS, V7X_LANES), lambda t, cnt: (t, 0)),
            ],
            out_specs=pl.BlockSpec(memory_space=pl.ANY),
            scratch_shapes=[pltpu.VMEM((tile * CHUNKS, V7X_LANES), F32),
                            pltpu.SMEM((N_EXPERT_GROUPS,), jnp.int32),
                            pltpu.SemaphoreType.DMA],
        ),
        out_shape=jax.ShapeDtypeStruct((n_tiles * tile * CHUNKS, V7X_LANES), F32),
        compiler_params=pltpu.CompilerParams(
            dimension_semantics=("arbitrary",),
            vmem_limit_bytes=_vmem_limit(4 * tile * D_MODEL * 4 + (1 << 20))),
        name="dispatch",
    )(counts, code, x_tm)


def _gather_kernel(cnt_ref, code_ref, xs_ref, o_ref, buf_ref, first_row_ref, sem):
    tile = o_ref.shape[0]
    firsts, _ = _group_tiles(cnt_ref, tile)
    for g in range(N_EXPERT_GROUPS):
        first_row_ref[g] = firsts[g] * tile

    def token_copy(r, p):
        return pltpu.make_async_copy(_token_rows(xs_ref, p), _token_rows(buf_ref, r), sem)

    def issue(i, carry):
        for j in range(ISSUE_UNROLL):
            r = i * ISSUE_UNROLL + j
            token_copy(r, _sorted_row(code_ref[0, r], first_row_ref)).start(priority=j % 2)
        return carry

    lax.fori_loop(0, tile // ISSUE_UNROLL, issue, 0)
    for _ in range(tile):
        token_copy(0, 0).wait()
    o_ref[...] = _load_token_major(buf_ref)


def _gather_call(xs, code, counts, n, tile):
    return pl.pallas_call(
        _gather_kernel,
        grid_spec=pltpu.PrefetchScalarGridSpec(
            num_scalar_prefetch=1,
            grid=(n // tile,),
            in_specs=[
                pl.BlockSpec((None, 1, tile), lambda t, cnt: (t, 0, 0), memory_space=pltpu.SMEM),
                pl.BlockSpec(memory_space=pl.ANY),
            ],
            out_specs=pl.BlockSpec((tile, D_MODEL), lambda t, cnt: (t, 0)),
            scratch_shapes=[pltpu.VMEM((tile * CHUNKS, V7X_LANES), F32),
                            pltpu.SMEM((N_EXPERT_GROUPS,), jnp.int32),
                            pltpu.SemaphoreType.DMA],
        ),
        out_shape=jax.ShapeDtypeStruct((n, D_MODEL), F32),
        compiler_params=pltpu.CompilerParams(
            dimension_semantics=("arbitrary",),
            vmem_limit_bytes=_vmem_limit(4 * tile * D_MODEL * 4 + (1 << 20))),
        name="gather",
    )(counts, code, xs)


def _tile_group(step, cnt_ref, tile):
    firsts, total = _group_tiles(cnt_ref, tile)
    group = jnp.int32(0)
    for g in range(1, N_EXPERT_GROUPS):
        group = group + (step >= firsts[g]).astype(jnp.int32)
    return group, step < total


def _moe_kernel(cnt_ref, x_ref, rw_ref, rb_ref, wgu_ref, wdn_ref, ln_g_ref, ln_b_ref, o_ref):
    tile = x_ref.shape[0] // CHUNKS
    _, valid = _tile_group(pl.program_id(0), cnt_ref, tile)

    @pl.when(jnp.logical_not(valid))
    def _():
        o_ref[...] = jnp.zeros_like(o_ref)

    @pl.when(valid)
    def _():
        lane = lax.broadcasted_iota(jnp.int32, (1, V7X_LANES), 1)
        sub = tile // MOE_SUBTILES
        def hidden(part):
            x = _load_token_major(x_ref, part * sub, sub)
            xb = x.astype(BF16)
            sj = _expert_slots(jax.nn.sigmoid(_dot(xb, rw_ref[...]) + rb_ref[...]))
            comb = _top2_weights(sj, lane == 0)
            hs = []
            for j in range(EXPERTS_PER_GROUP):
                gu = _dot(xb, wgu_ref[j])
                gate, up = gu[:, :D_EXPERT], gu[:, D_EXPERT:]
                hs.append((gate * jax.nn.sigmoid(gate) * up * comb[j][:, 0:1]).astype(BF16))
            return x, jnp.concatenate(hs, axis=-1)

        def finish(part, x, h):
            y = _dot(h, wdn_ref[...])
            _store_token_major(o_ref, _layer_norm(ALPHA * x + y, ln_g_ref[...], ln_b_ref[...]), part * sub)

        pending = {}
        for t in range(MOE_SUBTILES + 1):
            if t < MOE_SUBTILES:
                pending[t] = hidden(t)
            if t >= 1:
                finish(t - 1, *pending.pop(t - 1))


def _moe_call(xs, counts, l, w, tile):
    hid = EXPERTS_PER_GROUP * D_EXPERT
    n_tiles = xs.shape[0] // (tile * CHUNKS)

    def group_of(t, cnt):
        return _tile_group(t, cnt, tile)[0]

    vec = lambda m: pl.BlockSpec((None, 1, m), lambda t, cnt: (l, 0, 0), pipeline_mode=pl.Buffered(1))
    weight_bytes = 2 * 2 * (D_MODEL * 2 * hid + hid * D_MODEL)
    io_bytes = 2 * 2 * tile * D_MODEL * 4
    work_bytes = 10 * tile * D_MODEL * 4
    return pl.pallas_call(
        _moe_kernel,
        grid_spec=pltpu.PrefetchScalarGridSpec(
            num_scalar_prefetch=1,
            grid=(n_tiles,),
            in_specs=[
                pl.BlockSpec((tile * CHUNKS, V7X_LANES), lambda t, cnt: (t, 0)),
                pl.BlockSpec((None, D_MODEL, V7X_LANES), lambda t, cnt: (group_of(t, cnt), 0, 0)),
                pl.BlockSpec((None, 1, V7X_LANES), lambda t, cnt: (group_of(t, cnt), 0, 0)),
                pl.BlockSpec((None, EXPERTS_PER_GROUP, D_MODEL, 2 * D_EXPERT),
                             lambda t, cnt: (l * N_EXPERT_GROUPS + group_of(t, cnt), 0, 0, 0)),
                pl.BlockSpec((None, hid, D_MODEL),
                             lambda t, cnt: (l * N_EXPERT_GROUPS + group_of(t, cnt), 0, 0)),
                vec(D_MODEL), vec(D_MODEL),
            ],
            out_specs=pl.BlockSpec((tile * CHUNKS, V7X_LANES), lambda t, cnt: (t, 0)),
        ),
        out_shape=jax.ShapeDtypeStruct(xs.shape, F32),
        compiler_params=pltpu.CompilerParams(
            dimension_semantics=("arbitrary",),
            vmem_limit_bytes=_vmem_limit(weight_bytes + io_bytes + work_bytes)),
        name="moe",
    )(counts, xs, w["router_wg"], w["router_bg"], w["moe_wgu"], w["moe_wdn"], w["ln3_g"], w["ln3_b"])


def _prepare(p):
    depth = p["w_in"].shape[0]
    row = lambda a: a.reshape(depth, 1, -1)
    w = {k: p[k].astype(BF16) for k in ("w_in", "w_branch", "w_out", "xa_wq", "xa_wkv", "xa_wo")}
    for k in ("b_gate", "pool_scale", "cf_ln_g", "cf_ln_b", "gm_ln_g", "gm_ln_b",
              "ln1_g", "ln1_b", "ln2_g", "ln2_b", "ln3_g", "ln3_b"):
        w[k] = row(p[k])
    for k in ("sc_conv_w", "cf_conv_w", "gm_ws"):
        w[k] = p[k]
    eye = jnp.eye(N_GROUPS, dtype=F32)
    w["pool_w"] = jnp.einsum("lgcd,gh->lgchd", p["pool_w"], eye).reshape(depth, MIX_W, MIX_W).astype(BF16)
    w["gm_bias"] = jnp.repeat(jnp.swapaxes(p["gm_bs"], 1, 2), GROUP_W, axis=2)
    lanes = (jnp.arange(N_EXPERTS) % EXPERTS_PER_GROUP) * ROUTER_LANE_STRIDE + jnp.arange(N_EXPERTS) // EXPERTS_PER_GROUP
    rw = jnp.zeros((D_MODEL, V7X_LANES), F32).at[:, lanes].set(p["router_w"].astype(F32))
    rw_hi = rw.astype(BF16)
    rw_lo = (rw - rw_hi.astype(F32)).astype(BF16)
    w["router_w"] = jnp.concatenate([rw_hi, rw_lo], axis=1)
    w["router_b"] = jnp.zeros((1, V7X_LANES), F32).at[0, lanes].set(p["router_b"].astype(F32))
    slot = jnp.arange(EXPERTS_PER_GROUP) * ROUTER_LANE_STRIDE
    rwg = jnp.zeros((N_EXPERT_GROUPS, D_MODEL, V7X_LANES), F32).at[:, :, slot].set(
        jnp.transpose(p["router_w"].astype(F32).reshape(D_MODEL, N_EXPERT_GROUPS, EXPERTS_PER_GROUP), (1, 0, 2)))
    w["router_wg"] = rwg.astype(BF16)
    w["router_bg"] = jnp.zeros((N_EXPERT_GROUPS, 1, V7X_LANES), F32).at[:, 0, slot].set(
        p["router_b"].astype(F32).reshape(N_EXPERT_GROUPS, EXPERTS_PER_GROUP))
    w["moe_wgu"] = p["moe_w_gu"].astype(BF16).reshape(
        depth * N_EXPERT_GROUPS, EXPERTS_PER_GROUP, D_MODEL, 2 * D_EXPERT)
    w["moe_wdn"] = p["moe_w_down"].astype(BF16).reshape(
        depth * N_EXPERT_GROUPS, EXPERTS_PER_GROUP * D_EXPERT, D_MODEL)
    return w


def kernel(x, mem, w_in, b_gate, pool_w, pool_scale, sc_conv_w, cf_conv_w, cf_ln_g, cf_ln_b, gm_ln_g, gm_ln_b, gm_ws, gm_bs, w_branch, w_out, ln1_g, ln1_b, xa_wq, xa_wkv, xa_wo, ln2_g, ln2_b, router_w, router_b, moe_w_gu, moe_w_down, ln3_g, ln3_b):
    params = dict(w_in=w_in, b_gate=b_gate, pool_w=pool_w, pool_scale=pool_scale, sc_conv_w=sc_conv_w,
                  cf_conv_w=cf_conv_w, cf_ln_g=cf_ln_g, cf_ln_b=cf_ln_b, gm_ln_g=gm_ln_g, gm_ln_b=gm_ln_b,
                  gm_ws=gm_ws, gm_bs=gm_bs, w_branch=w_branch, w_out=w_out, ln1_g=ln1_g, ln1_b=ln1_b,
                  xa_wq=xa_wq, xa_wkv=xa_wkv, xa_wo=xa_wo, ln2_g=ln2_g, ln2_b=ln2_b, router_w=router_w,
                  router_b=router_b, moe_w_gu=moe_w_gu, moe_w_down=moe_w_down, ln3_g=ln3_g, ln3_b=ln3_b)
    w = _prepare(params)
    bsz, seq, d = x.shape
    n = bsz * seq
    k_all, v_all = _kv_call(mem, w["xa_wkv"])
    tile = 512
    assert n <= 1 << RANK_BITS
    sorted_input = None
    for l in range(DEPTH):
        x = _mixer_call(x, l, w, tile, sorted_input)
        x2, code, cnt = _xattn_call(x, k_all, v_all, l, w, tile)
        counts = cnt[0, :N_EXPERT_GROUPS]
        x = _moe_call(_dispatch_call(x2, code, counts, tile), counts, l, w, tile)
        sorted_input = (code, counts, (bsz, seq))
    return _gather_call(x, code, counts, n, tile).reshape(bsz, seq, d)
```

```python
import functools

import jax
import jax.numpy as jnp
from jax import lax
from jax.experimental import pallas as pl
from jax.experimental.pallas import tpu as pltpu

F32 = jnp.float32
BF16 = jnp.bfloat16

D_MODEL = 1024
DEPTH = 4
MIX_W = 256
N_GROUPS = 4
GROUP_W = 64
SHORT_CONV_K = 3
CONFORMER_CONV_K = 31
CHUNK = 128
COL_SCONV = MIX_W
COL_CONF = 4 * MIX_W
COL_GMLP = 6 * MIX_W
COL_GATE = 8 * MIX_W
N_BRANCHES = 4
IN_COLS = COL_GATE + N_BRANCHES * D_MODEL
MEM_LEN = 256
N_XA_HEADS = 4
XA_HEAD_DIM = D_MODEL // N_XA_HEADS
N_EXPERTS = 16
N_EXPERT_GROUPS = 4
EXPERTS_PER_GROUP = 4
D_EXPERT = 256
ALPHA = (2 * DEPTH) ** 0.25
LN_EPS = 1e-5

V7X_SUBLANES = 8
V7X_LANES = 128
V7X_VMEM_BYTES = 64 * 1024 * 1024

HALO = 32
SC_HALO = V7X_SUBLANES
GATE_SLAB = 512
ROUTER_LANE_STRIDE = 32
RANK_BITS = 16
ISSUE_UNROLL = 8
MOE_SUBTILES = 2
XATTN_SUBTILES = 2


def _layer_norm(y, g, b):
    mu = jnp.mean(y, axis=-1, keepdims=True)
    d = y - mu
    var = jnp.mean(d * d, axis=-1, keepdims=True)
    return d * lax.rsqrt(var + LN_EPS) * g + b


def _dot(a, b):
    return jnp.dot(a, b, preferred_element_type=F32)


CHUNKS = D_MODEL // V7X_LANES


def _store_token_major(ref, value, first=0):
    count = value.shape[0]
    for k in range(CHUNKS):
        ref[pl.ds(first * CHUNKS + k, count, stride=CHUNKS), :] = value[:, k * V7X_LANES:(k + 1) * V7X_LANES]


def _load_token_major(ref, first=0, count=None):
    if count is None:
        count = ref.shape[0] // CHUNKS - first
    return jnp.concatenate(
        [ref[pl.ds(first * CHUNKS + k, count, stride=CHUNKS), :] for k in range(CHUNKS)], axis=1)


def _const_spec(shape, index):
    return pl.BlockSpec(shape, lambda *_: index, pipeline_mode=pl.Buffered(1))


def _vmem_limit(nbytes):
    return int(min(V7X_VMEM_BYTES - 4 * 1024 * 1024, nbytes))


def _kv_kernel(mem_ref, wkv_ref, k_ref, v_ref):
    kv = _dot(mem_ref[...].astype(BF16), wkv_ref[...])
    k_ref[...] = kv[:, :D_MODEL].astype(BF16)
    v_ref[...] = kv[:, D_MODEL:].astype(BF16)


def _kv_call(mem, wkv_bf16):
    bsz = mem.shape[0]
    out = jax.ShapeDtypeStruct((DEPTH, bsz, MEM_LEN, D_MODEL), BF16)
    return pl.pallas_call(
        _kv_kernel,
        grid=(DEPTH, bsz),
        in_specs=[
            pl.BlockSpec((None, MEM_LEN, D_MODEL), lambda l, b: (b, 0, 0)),
            pl.BlockSpec((None, D_MODEL, 2 * D_MODEL), lambda l, b: (l, 0, 0)),
        ],
        out_specs=[
            pl.BlockSpec((None, None, MEM_LEN, D_MODEL), lambda l, b: (l, b, 0, 0)),
            pl.BlockSpec((None, None, MEM_LEN, D_MODEL), lambda l, b: (l, b, 0, 0)),
        ],
        out_shape=[out, out],
        compiler_params=pltpu.CompilerParams(dimension_semantics=("arbitrary", "arbitrary")),
        name="mem_kv",
    )(mem, wkv_bf16)


def _prefetched_tokens(cnt_ref, code_first_ref, code_next_ref, xs_ref, buf_ref, first_row_ref, sems, tile):
    step = pl.program_id(0) * pl.num_programs(1) + pl.program_id(1)
    last = pl.num_programs(0) * pl.num_programs(1) - 1
    slot = lax.rem(step, 2)
    firsts, _ = _group_tiles(cnt_ref, tile)
    for g in range(N_EXPERT_GROUPS):
        first_row_ref[g] = firsts[g] * tile

    def token_copy(r, p, to_slot):
        return pltpu.make_async_copy(_token_rows(xs_ref, p), _token_rows(buf_ref.at[to_slot], r), sems.at[to_slot])

    def fetch(code_ref, to_slot):
        def issue(i, carry):
            for j in range(ISSUE_UNROLL):
                r = i * ISSUE_UNROLL + j
                token_copy(r, _sorted_row(code_ref[0, r], first_row_ref), to_slot).start(priority=j % 2)
            return carry

        lax.fori_loop(0, tile // ISSUE_UNROLL, issue, 0)

    @pl.when(step == 0)
    def _():
        fetch(code_first_ref, 0)

    for _ in range(tile):
        token_copy(0, 0, slot).wait()
    x = _load_token_major(buf_ref.at[slot])

    for r in range(tile):
        token_copy(r, _sorted_row(code_next_ref[0, r], first_row_ref), 1 - slot).start(priority=r % 2)

    def drain_last():
        @pl.when(step == last)
        def _():
            for _ in range(tile):
                token_copy(0, 0, 1 - slot).wait()

    return x, drain_last


def _mixer_kernel(*refs, gathered):
    if gathered:
        cnt_ref, code_first_ref, code_next_ref, xs_ref, *refs = refs
        *refs, tok_buf, first_row_ref, sems = refs
    else:
        x_ref, *refs = refs
    (w_in_ref, b_gate_ref, pool_w_ref, pool_scale_ref, sc_w_ref, cf_w_ref,
     cf_g_ref, cf_b_ref, gm_g_ref, gm_b_ref, gm_ws_ref, gm_bias_ref, w_br_ref, w_out_ref,
     ln_g_ref, ln_b_ref, o_ref,
     pool_e, pool_a, pool_b, pool_c, sc_e, cf_e, cf_y, gates, br_a, br_b, br_c, br_d) = refs
    s = pl.program_id(1)
    tile = o_ref.shape[0]
    end = tile + HALO

    @pl.when(s == 0)
    def _():
        pool_e[0:HALO, :] = jnp.zeros((HALO, MIX_W), F32)
        sc_e[0:SC_HALO, :] = jnp.zeros((SC_HALO, MIX_W), F32)
        cf_e[0:HALO, :] = jnp.zeros((HALO, MIX_W), F32)

    if gathered:
        x, drain_last = _prefetched_tokens(
            cnt_ref, code_first_ref, code_next_ref, xs_ref, tok_buf, first_row_ref, sems, tile)
    else:
        x = x_ref[...]
    xb = x.astype(BF16)

    def proj(lo, hi):
        return _dot(xb, w_in_ref[:, lo:hi])

    lane = lax.broadcasted_iota(jnp.int32, (1, MIX_W), 1)
    lane_group = lane // GROUP_W

    z = proj(0, MIX_W)
    pool_e[HALO:end, :] = z
    pb = proj(COL_SCONV, COL_CONF)
    bg = pb[:, MIX_W:2 * MIX_W]
    u = pb[:, 2 * MIX_W:] * pb[:, :MIX_W]
    sc_e[SC_HALO:SC_HALO + tile, :] = u
    pc = proj(COL_CONF, COL_GMLP)
    cf_e[HALO:end, :] = pc[:, :MIX_W] * jax.nn.sigmoid(pc[:, MIX_W:])
    pd = jax.nn.gelu(proj(COL_GMLP, COL_GATE))
    gu = pd[:, :MIX_W]
    gv = _layer_norm(pd[:, MIX_W:], gm_g_ref[...], gm_b_ref[...]).astype(BF16)
    row = lax.broadcasted_iota(jnp.int32, (CHUNK, CHUNK), 0)
    col = lax.broadcasted_iota(jnp.int32, (CHUNK, CHUNK), 1)
    ws_cat = jnp.concatenate(
        [jnp.where(row >= col, gm_ws_ref[g], 0.0).astype(BF16) for g in range(N_GROUPS)], axis=1)

    def pool_unit():
        pool_a[8:end, :] = pool_e[8:end, :] + pool_e[7:end - 1, :]
        pool_b[16:end, :] = pool_a[16:end, :] + pool_a[14:end - 2, :]
        pool_c[24:end, :] = pool_b[24:end, :] + pool_b[20:end - 4, :]
        s16 = pool_c[HALO:end, :] + pool_c[HALO - 8:end - 8, :]
        ssum = jnp.where(lane_group == 0, pool_a[HALO:end, :],
                         jnp.where(lane_group == 1, pool_b[HALO:end, :],
                                   jnp.where(lane_group == 2, pool_c[HALO:end, :], s16)))
        win = jnp.where(lane_group == 0, 2.0,
                        jnp.where(lane_group == 1, 4.0, jnp.where(lane_group == 2, 8.0, 16.0))).astype(F32)
        count = (s * tile + lax.broadcasted_iota(jnp.int32, (tile, 1), 0) + 1).astype(F32)
        a = ssum / jnp.minimum(count, win) - z
        pool_e[0:HALO, :] = pool_e[tile:end, :]
        br_a[...] = (_dot(a.astype(BF16), pool_w_ref[...]) * pool_scale_ref[...]).astype(BF16)

    def sconv_unit():
        conv = (sc_w_ref[0:1, :] * sc_e[SC_HALO - 2:SC_HALO - 2 + tile, :]
                + sc_w_ref[1:2, :] * sc_e[SC_HALO - 1:SC_HALO - 1 + tile, :]
                + sc_w_ref[2:3, :] * u)
        br_b[...] = (bg * conv).astype(BF16)
        sc_e[0:SC_HALO, :] = sc_e[tile:tile + SC_HALO, :]

    first = HALO - (CONFORMER_CONV_K - 1)

    def conv_phase(phase):
        rows = tile if phase == 0 else tile + V7X_SUBLANES
        acc = None
        for q in range((first + CONFORMER_CONV_K - 1) // V7X_SUBLANES + 1):
            k = q * V7X_SUBLANES + phase - first
            if 0 <= k < CONFORMER_CONV_K:
                term = cf_w_ref[k:k + 1, :] * cf_e[q * V7X_SUBLANES:q * V7X_SUBLANES + rows, :]
                acc = term if acc is None else acc + term
        cf_y[phase, 0:rows, :] = acc

    def conv_finish(part, parts):
        n_rows = tile // parts
        lo = part * n_rows
        acc = cf_y[0, lo:lo + n_rows, :]
        for phase in range(1, V7X_SUBLANES):
            acc = acc + cf_y[phase, lo + phase:lo + phase + n_rows, :]
        cn = _layer_norm(acc, cf_g_ref[...], cf_b_ref[...])
        br_c[lo:lo + n_rows, :] = (cn * jax.nn.sigmoid(cn)).astype(BF16)
        if part == parts - 1:
            cf_e[0:HALO, :] = cf_e[tile:end, :]

    def gmlp_unit(c):
        rows = slice(c * CHUNK, (c + 1) * CHUNK)
        vc = gv[rows, :]
        vblk = jnp.concatenate([jnp.where(lane_group == g, vc, jnp.zeros_like(vc)) for g in range(N_GROUPS)],
                               axis=0)
        sv = _dot(ws_cat, vblk) + gm_bias_ref[...]
        br_d[rows, :] = (gu[rows, :] * sv).astype(BF16)

    n_chunks = tile // CHUNK
    units = ([pool_unit, sconv_unit]
             + [functools.partial(conv_phase, p) for p in range(V7X_SUBLANES)]
             + [functools.partial(gmlp_unit, c) for c in range(n_chunks)]
             + [functools.partial(conv_finish, p, 4) for p in range(4)])
    n_slabs = N_BRANCHES * D_MODEL // GATE_SLAB
    per_slab = -(-len(units) // n_slabs)
    for r in range(n_slabs):
        lo = r * GATE_SLAB
        gates[:, lo:lo + GATE_SLAB] = jax.nn.sigmoid(
            proj(COL_GATE + lo, COL_GATE + lo + GATE_SLAB) + b_gate_ref[:, lo:lo + GATE_SLAB])
        for unit in units[r * per_slab:(r + 1) * per_slab]:
            unit()

    merged = None
    for i, br in enumerate((br_a, br_b, br_c, br_d)):
        term = gates[:, i * D_MODEL:(i + 1) * D_MODEL] * _dot(br[...], w_br_ref[i])
        merged = term if merged is None else merged + term
    mix = _dot(merged.astype(BF16), w_out_ref[...])
    o_ref[...] = _layer_norm(ALPHA * x + mix, ln_g_ref[...], ln_b_ref[...])
    if gathered:
        drain_last()


def _mixer_call(x, l, w, tile, sorted_input=None):
    gathered = sorted_input is not None
    if gathered:
        code, counts, (bsz, seq) = sorted_input
    else:
        bsz, seq, _ = x.shape
    steps = seq // tile
    rows = tile + HALO
    vec = lambda n: _const_spec((None, 1, n), (l, 0, 0))
    in_specs = [
        _const_spec((None, D_MODEL, IN_COLS), (l, 0, 0)),
        vec(N_BRANCHES * D_MODEL),
        _const_spec((None, MIX_W, MIX_W), (l, 0, 0)),
        vec(MIX_W),
        _const_spec((None, SHORT_CONV_K, MIX_W), (l, 0, 0)),
        _const_spec((None, CONFORMER_CONV_K, MIX_W), (l, 0, 0)),
        vec(MIX_W), vec(MIX_W), vec(MIX_W), vec(MIX_W),
        _const_spec((None, N_GROUPS, CHUNK, CHUNK), (l, 0, 0, 0)),
        _const_spec((None, CHUNK, MIX_W), (l, 0, 0)),
        _const_spec((None, N_BRANCHES, MIX_W, D_MODEL), (l, 0, 0, 0)),
        _const_spec((None, D_MODEL, D_MODEL), (l, 0, 0)),
        vec(D_MODEL), vec(D_MODEL),
    ]
    scratch = [
        pltpu.VMEM((rows, MIX_W), F32), pltpu.VMEM((rows, MIX_W), F32),
        pltpu.VMEM((rows, MIX_W), F32), pltpu.VMEM((rows, MIX_W), F32),
        pltpu.VMEM((tile + 2 * SC_HALO, MIX_W), F32),
        pltpu.VMEM((rows, MIX_W), F32),
        pltpu.VMEM((V7X_SUBLANES, tile + V7X_SUBLANES, MIX_W), F32),
        pltpu.VMEM((tile, N_BRANCHES * D_MODEL), F32),
        pltpu.VMEM((tile, MIX_W), BF16), pltpu.VMEM((tile, MIX_W), BF16),
        pltpu.VMEM((tile, MIX_W), BF16), pltpu.VMEM((tile, MIX_W), BF16),
    ]
    weights = (w["w_in"], w["b_gate"], w["pool_w"], w["pool_scale"], w["sc_conv_w"], w["cf_conv_w"],
               w["cf_ln_g"], w["cf_ln_b"], w["gm_ln_g"], w["gm_ln_b"], w["gm_ws"], w["gm_bias"],
               w["w_branch"], w["w_out"], w["ln1_g"], w["ln1_b"])
    if gathered:
        last = bsz * steps - 1
        in_specs = [
            pl.BlockSpec((None, 1, tile), lambda b, s, cnt: (0, 0, 0), memory_space=pltpu.SMEM),
            pl.BlockSpec((None, 1, tile), lambda b, s, cnt: (jnp.minimum(b * steps + s + 1, last), 0, 0),
                         memory_space=pltpu.SMEM),
            pl.BlockSpec(memory_space=pl.ANY),
        ] + in_specs
        scratch = scratch + [pltpu.VMEM((2, tile * CHUNKS, V7X_LANES), F32),
                             pltpu.SMEM((N_EXPERT_GROUPS,), jnp.int32),
                             pltpu.SemaphoreType.DMA((2,))]
        operands = (counts, code, code, x) + weights
    else:
        in_specs = [pl.BlockSpec((None, tile, D_MODEL), lambda b, s: (b, s, 0))] + in_specs
        operands = (x,) + weights
    weight_bytes = 2 * (D_MODEL * IN_COLS + N_BRANCHES * MIX_W * D_MODEL + D_MODEL * D_MODEL + MIX_W * MIX_W)
    io_bytes = 2 * 2 * tile * D_MODEL * 4
    work_bytes = 20 * tile * D_MODEL * 4
    return pl.pallas_call(
        functools.partial(_mixer_kernel, gathered=gathered),
        grid_spec=pltpu.PrefetchScalarGridSpec(
            num_scalar_prefetch=1 if gathered else 0,
            grid=(bsz, steps),
            in_specs=in_specs,
            out_specs=pl.BlockSpec((None, tile, D_MODEL), lambda b, s, *_: (b, s, 0)),
            scratch_shapes=scratch,
        ),
        out_shape=jax.ShapeDtypeStruct((bsz, seq, D_MODEL), F32),
        compiler_params=pltpu.CompilerParams(
            dimension_semantics=("arbitrary", "arbitrary"),
            vmem_limit_bytes=_vmem_limit(weight_bytes + io_bytes + work_bytes)),
        name="mixer",
    )(*operands)


def _router_scores(x, rw_ref, rb_ref):
    x_hi = x.astype(BF16)
    x_lo = (x - x_hi.astype(F32)).astype(BF16)
    by_hi = _dot(x_hi, rw_ref[...])
    by_lo = _dot(x_lo, rw_ref[:, :V7X_LANES])
    return jax.nn.sigmoid(by_hi[:, :V7X_LANES] + by_hi[:, V7X_LANES:] + by_lo + rb_ref[...])


def _expert_slots(scores):
    return [scores if j == 0 else pltpu.roll(scores, V7X_LANES - j * ROUTER_LANE_STRIDE, axis=1)
            for j in range(EXPERTS_PER_GROUP)]


def _top2_weights(sj, keep):
    picked = []
    for j in range(EXPERTS_PER_GROUP):
        rank = jnp.zeros_like(sj[j])
        for i in range(EXPERTS_PER_GROUP):
            if i < j:
                rank = rank + (sj[i] >= sj[j]).astype(F32)
            elif i > j:
                rank = rank + (sj[i] > sj[j]).astype(F32)
        picked.append(jnp.where(keep & (rank < 2.0), sj[j], 0.0))
    denom = jnp.sum(picked[0] + picked[1] + picked[2] + picked[3], axis=-1, keepdims=True)
    return [p / denom for p in picked]


def _xattn_kernel(x_ref, k_ref, v_ref, wq_ref, wo_ref, ln_g_ref, ln_b_ref, rw_ref, rb_ref,
                  o_ref, pos_ref, cnt_ref, base_ref):
    @pl.when((pl.program_id(0) == 0) & (pl.program_id(1) == 0))
    def _():
        base_ref[...] = jnp.zeros_like(base_ref)

    tile = x_ref.shape[0]
    sub = tile // XATTN_SUBTILES
    lane = lax.broadcasted_iota(jnp.int32, (1, V7X_LANES), 1)
    r = lax.broadcasted_iota(jnp.int32, (sub, sub), 0)
    c = lax.broadcasted_iota(jnp.int32, (sub, sub), 1)
    before = (r > c).astype(BF16)
    head = lambda h: slice(h * XA_HEAD_DIM, (h + 1) * XA_HEAD_DIM)

    def scores(part):
        x = x_ref[part * sub:(part + 1) * sub, :]
        q = _dot(x.astype(BF16), wq_ref[...]).astype(BF16)
        return x, [lax.dot_general(q[:, head(h)], k_ref[:, head(h)], (((1,), (1,)), ((), ())),
                                   preferred_element_type=F32) * (XA_HEAD_DIM ** -0.5)
                   for h in range(N_XA_HEADS)]

    def softmax(sc):
        e = jnp.exp(sc - jnp.max(sc, axis=-1, keepdims=True))
        return (e / jnp.sum(e, axis=-1, keepdims=True)).astype(BF16)

    def attend(part, x, probs):
        o = jnp.concatenate([_dot(probs[h], v_ref[:, head(h)]).astype(BF16) for h in range(N_XA_HEADS)], axis=-1)
        x2 = _layer_norm(ALPHA * x + _dot(o, wo_ref[...]), ln_g_ref[...], ln_b_ref[...])
        _store_token_major(o_ref, x2, part * sub)
        return _router_scores(x2, rw_ref, rb_ref)

    def select_group(scores_):
        sj = _expert_slots(scores_)
        hi01, lo01 = jnp.maximum(sj[0], sj[1]), jnp.minimum(sj[0], sj[1])
        hi23, lo23 = jnp.maximum(sj[2], sj[3]), jnp.minimum(sj[2], sj[3])
        top1 = jnp.maximum(hi01, hi23)
        top2 = jnp.maximum(jnp.minimum(hi01, hi23), jnp.maximum(lo01, lo23))
        gscore = jnp.where(lane < N_EXPERT_GROUPS, top1 + top2, -1.0)
        gmax = jnp.max(gscore, axis=-1, keepdims=True)
        sel = jnp.min(jnp.where(gscore == gmax, lane, V7X_LANES), axis=-1, keepdims=True)
        return sel, (lane == sel).astype(F32)

    def place(part, sel, onehot, base):
        earlier = _dot(before, onehot.astype(BF16))
        rank = jnp.sum(onehot * (earlier + base), axis=-1, keepdims=True)
        code = sel.astype(F32) * float(1 << RANK_BITS) + rank
        code_rows = jnp.transpose(jnp.broadcast_to(code, (sub, V7X_LANES)))
        pos_ref[:, part * sub:(part + 1) * sub] = code_rows[0:1, :].astype(jnp.int32)
        return base + jnp.sum(onehot, axis=0, keepdims=True)

    base = base_ref[...]
    stage1, stage2, stage3 = {}, {}, {}
    for t in range(XATTN_SUBTILES + 3):
        if t < XATTN_SUBTILES:
            stage1[t] = scores(t)
        if 0 <= t - 1 < XATTN_SUBTILES:
            x, sc = stage1.pop(t - 1)
            stage2[t - 1] = attend(t - 1, x, [softmax(s) for s in sc])
        if 0 <= t - 2 < XATTN_SUBTILES:
            stage3[t - 2] = select_group(stage2.pop(t - 2))
        if 0 <= t - 3 < XATTN_SUBTILES:
            base = place(t - 3, *stage3.pop(t - 3), base)
    base_ref[...] = base
    cnt_ref[...] = jnp.broadcast_to(base, (V7X_SUBLANES, V7X_LANES)).astype(jnp.int32)


def _xattn_call(x, k, v, l, w, tile):
    bsz, seq, _ = x.shape
    steps = seq // tile
    vec = lambda n: _const_spec((None, 1, n), (l, 0, 0))
    weight_bytes = 2 * 2 * D_MODEL * D_MODEL + 2 * 2 * 2 * MEM_LEN * D_MODEL
    io_bytes = 2 * 2 * tile * D_MODEL * 4
    work_bytes = 10 * tile * D_MODEL * 4
    return pl.pallas_call(
        _xattn_kernel,
        grid=(bsz, steps),
        in_specs=[
            pl.BlockSpec((None, tile, D_MODEL), lambda b, s: (b, s, 0)),
            pl.BlockSpec((None, None, MEM_LEN, D_MODEL), lambda b, s: (l, b, 0, 0)),
            pl.BlockSpec((None, None, MEM_LEN, D_MODEL), lambda b, s: (l, b, 0, 0)),
            _const_spec((None, D_MODEL, D_MODEL), (l, 0, 0)),
            _const_spec((None, D_MODEL, D_MODEL), (l, 0, 0)),
            vec(D_MODEL), vec(D_MODEL),
            _const_spec((D_MODEL, 2 * V7X_LANES), (0, 0)),
            _const_spec((1, V7X_LANES), (0, 0)),
        ],
        out_specs=[
            pl.BlockSpec((tile * CHUNKS, V7X_LANES), lambda b, s: (b * steps + s, 0)),
            pl.BlockSpec((None, 1, tile), lambda b, s: (b * steps + s, 0, 0)),
            pl.BlockSpec((V7X_SUBLANES, V7X_LANES), lambda b, s: (0, 0)),
        ],
        out_shape=[
            jax.ShapeDtypeStruct((bsz * seq * CHUNKS, V7X_LANES), F32),
            jax.ShapeDtypeStruct((bsz * steps, 1, tile), jnp.int32),
            jax.ShapeDtypeStruct((V7X_SUBLANES, V7X_LANES), jnp.int32),
        ],
        scratch_shapes=[pltpu.VMEM((1, V7X_LANES), F32)],
        compiler_params=pltpu.CompilerParams(
            dimension_semantics=("arbitrary", "arbitrary"),
            vmem_limit_bytes=_vmem_limit(weight_bytes + io_bytes + work_bytes)),
        name="xattn",
    )(x, k, v, w["xa_wq"], w["xa_wo"], w["ln2_g"], w["ln2_b"], w["router_w"], w["router_b"])


def _token_rows(ref, t):
    return ref.at[pl.ds(pl.multiple_of(t * CHUNKS, CHUNKS), CHUNKS)]


def _group_tiles(cnt_ref, tile):
    shift = tile.bit_length() - 1
    assert tile == 1 << shift
    firsts, total = [], jnp.int32(0)
    for g in range(N_EXPERT_GROUPS):
        firsts.append(total)
        total = total + lax.shift_right_logical(cnt_ref[g] + (tile - 1), shift)
    return firsts, total


def _sorted_row(code, first_row_ref):
    group = lax.shift_right_logical(code, RANK_BITS)
    return first_row_ref[group] + (code & ((1 << RANK_BITS) - 1))


def _dispatch_kernel(cnt_ref, code_ref, x_ref, xs_ref, zero_ref, first_row_ref, sem):
    tile = x_ref.shape[0] // CHUNKS
    tile_rows = tile * CHUNKS
    n_tiles = xs_ref.shape[0] // tile_rows
    firsts, total = _group_tiles(cnt_ref, tile)
    for g in range(N_EXPERT_GROUPS):
        first_row_ref[g] = firsts[g] * tile

    def clear_tile(t):
        fill = pltpu.make_async_copy(
            zero_ref, xs_ref.at[pl.ds(pl.multiple_of(t * tile_rows, tile_rows), tile_rows)], sem)
        fill.start()
        fill.wait()

    @pl.when(pl.program_id(0) == 0)
    def _():
        zero_ref[...] = jnp.zeros_like(zero_ref)
        ends = firsts[1:] + [total]
        for g in range(N_EXPERT_GROUPS):
            @pl.when(cnt_ref[g] > 0)
            def _():
                clear_tile(ends[g] - 1)
        for t in range(n_tiles - N_EXPERT_GROUPS, n_tiles):
            @pl.when(total <= t)
            def _():
                clear_tile(t)

    def token_copy(r, p):
        return pltpu.make_async_copy(_token_rows(x_ref, r), _token_rows(xs_ref, p), sem)

    def issue(i, carry):
        for j in range(ISSUE_UNROLL):
            r = i * ISSUE_UNROLL + j
            token_copy(r, _sorted_row(code_ref[0, r], first_row_ref)).start(priority=j % 2)
        return carry

    lax.fori_loop(0, tile // ISSUE_UNROLL, issue, 0)
    for _ in range(tile):
        token_copy(0, 0).wait()


def _dispatch_call(x_tm, code, counts, tile):
    n = x_tm.shape[0] // CHUNKS
    n_tiles = n // tile + N_EXPERT_GROUPS
    return pl.pallas_call(
        _dispatch_kernel,
        grid_spec=pltpu.PrefetchScalarGridSpec(
            num_scalar_prefetch=1,
            grid=(n // tile,),
            in_specs=[
                pl.BlockSpec((None, 1, tile), lambda t, cnt: (t, 0, 0), memory_space=pltpu.SMEM),
                pl.BlockSpec((tile * CHUNKS, V7X_LANES), lambda t, cnt: (t, 0)),
            ],
            out_specs=pl.BlockSpec(memory_space=pl.ANY),
            scratch_shapes=[pltpu.VMEM((tile * CHUNKS, V7X_LANES), F32),
                            pltpu.SMEM((N_EXPERT_GROUPS,), jnp.int32),
                            pltpu.SemaphoreType.DMA],
        ),
        out_shape=jax.ShapeDtypeStruct((n_tiles * tile * CHUNKS, V7X_LANES), F32),
        compiler_params=pltpu.CompilerParams(
            dimension_semantics=("arbitrary",),
            vmem_limit_bytes=_vmem_limit(4 * tile * D_MODEL * 4 + (1 << 20))),
        name="dispatch",
    )(counts, code, x_tm)


def _gather_kernel(cnt_ref, code_ref, xs_ref, o_ref, buf_ref, first_row_ref, sem):
    tile = o_ref.shape[0]
    firsts, _ = _group_tiles(cnt_ref, tile)
    for g in range(N_EXPERT_GROUPS):
        first_row_ref[g] = firsts[g] * tile

    def token_copy(r, p):
        return pltpu.make_async_copy(_token_rows(xs_ref, p), _token_rows(buf_ref, r), sem)

    def issue(i, carry):
        for j in range(ISSUE_UNROLL):
            r = i * ISSUE_UNROLL + j
            token_copy(r, _sorted_row(code_ref[0, r], first_row_ref)).start(priority=j % 2)
        return carry

    lax.fori_loop(0, tile // ISSUE_UNROLL, issue, 0)
    for _ in range(tile):
        token_copy(0, 0).wait()
    o_ref[...] = _load_token_major(buf_ref)


def _gather_call(xs, code, counts, n, tile):
    return pl.pallas_call(
        _gather_kernel,
        grid_spec=pltpu.PrefetchScalarGridSpec(
            num_scalar_prefetch=1,
            grid=(n // tile,),
            in_specs=[
                pl.BlockSpec((None, 1, tile), lambda t, cnt: (t, 0, 0), memory_space=pltpu.SMEM),
                pl.BlockSpec(memory_space=pl.ANY),
            ],
            out_specs=pl.BlockSpec((tile, D_MODEL), lambda t, cnt: (t, 0)),
            scratch_shapes=[pltpu.VMEM((tile * CHUNKS, V7X_LANES), F32),
                            pltpu.SMEM((N_EXPERT_GROUPS,), jnp.int32),
                            pltpu.SemaphoreType.DMA],
        ),
        out_shape=jax.ShapeDtypeStruct((n, D_MODEL), F32),
        compiler_params=pltpu.CompilerParams(
            dimension_semantics=("arbitrary",),
            vmem_limit_bytes=_vmem_limit(4 * tile * D_MODEL * 4 + (1 << 20))),
        name="gather",
    )(counts, code, xs)


def _tile_group(step, cnt_ref, tile):
    firsts, total = _group_tiles(cnt_ref, tile)
    group = jnp.int32(0)
    for g in range(1, N_EXPERT_GROUPS):
        group = group + (step >= firsts[g]).astype(jnp.int32)
    return group, step < total


def _moe_kernel(cnt_ref, x_ref, rw_ref, rb_ref, wgu_f32_ref, wdn_f32_ref, ln_g_ref, ln_b_ref, o_ref,
                wgu_ref, wdn_ref):
    tile = x_ref.shape[0] // CHUNKS
    step = pl.program_id(0)
    group, valid = _tile_group(step, cnt_ref, tile)
    previous, _ = _tile_group(step - 1, cnt_ref, tile)

    @pl.when((step == 0) | (group != previous))
    def _():
        for j in range(EXPERTS_PER_GROUP):
            wgu_ref[j] = wgu_f32_ref[j].astype(BF16)
        wdn_ref[...] = wdn_f32_ref[...].astype(BF16)

    @pl.when(jnp.logical_not(valid))
    def _():
        o_ref[...] = jnp.zeros_like(o_ref)

    @pl.when(valid)
    def _():
        lane = lax.broadcasted_iota(jnp.int32, (1, V7X_LANES), 1)
        sub = tile // MOE_SUBTILES
        def hidden(part):
            x = _load_token_major(x_ref, part * sub, sub)
            xb = x.astype(BF16)
            sj = _expert_slots(jax.nn.sigmoid(_dot(xb, rw_ref[...]) + rb_ref[...]))
            comb = _top2_weights(sj, lane == 0)
            hs = []
            for j in range(EXPERTS_PER_GROUP):
                gu = _dot(xb, wgu_ref[j])
                gate, up = gu[:, :D_EXPERT], gu[:, D_EXPERT:]
                hs.append((gate * jax.nn.sigmoid(gate) * up * comb[j][:, 0:1]).astype(BF16))
            return x, jnp.concatenate(hs, axis=-1)

        def finish(part, x, h):
            y = _dot(h, wdn_ref[...])
            _store_token_major(o_ref, _layer_norm(ALPHA * x + y, ln_g_ref[...], ln_b_ref[...]), part * sub)

        pending = {}
        for t in range(MOE_SUBTILES + 1):
            if t < MOE_SUBTILES:
                pending[t] = hidden(t)
            if t >= 1:
                finish(t - 1, *pending.pop(t - 1))


def _moe_call(xs, counts, l, w, tile):
    hid = EXPERTS_PER_GROUP * D_EXPERT
    n_tiles = xs.shape[0] // (tile * CHUNKS)

    def group_of(t, cnt):
        return _tile_group(t, cnt, tile)[0]

    vec = lambda m: pl.BlockSpec((None, 1, m), lambda t, cnt: (l, 0, 0), pipeline_mode=pl.Buffered(1))
    group_weights = D_MODEL * 2 * hid + hid * D_MODEL
    weight_bytes = 2 * 4 * group_weights + 2 * group_weights
    io_bytes = 2 * 2 * tile * D_MODEL * 4
    work_bytes = 10 * tile * D_MODEL * 4
    return pl.pallas_call(
        _moe_kernel,
        grid_spec=pltpu.PrefetchScalarGridSpec(
            num_scalar_prefetch=1,
            grid=(n_tiles,),
            in_specs=[
                pl.BlockSpec((tile * CHUNKS, V7X_LANES), lambda t, cnt: (t, 0)),
                pl.BlockSpec((None, D_MODEL, V7X_LANES), lambda t, cnt: (group_of(t, cnt), 0, 0)),
                pl.BlockSpec((None, 1, V7X_LANES), lambda t, cnt: (group_of(t, cnt), 0, 0)),
                pl.BlockSpec((None, EXPERTS_PER_GROUP, D_MODEL, 2 * D_EXPERT),
                             lambda t, cnt: (l * N_EXPERT_GROUPS + group_of(t, cnt), 0, 0, 0)),
                pl.BlockSpec((None, hid, D_MODEL),
                             lambda t, cnt: (l * N_EXPERT_GROUPS + group_of(t, cnt), 0, 0)),
                vec(D_MODEL), vec(D_MODEL),
            ],
            out_specs=pl.BlockSpec((tile * CHUNKS, V7X_LANES), lambda t, cnt: (t, 0)),
            scratch_shapes=[pltpu.VMEM((EXPERTS_PER_GROUP, D_MODEL, 2 * D_EXPERT), BF16),
                            pltpu.VMEM((hid, D_MODEL), BF16)],
        ),
        out_shape=jax.ShapeDtypeStruct(xs.shape, F32),
        compiler_params=pltpu.CompilerParams(
            dimension_semantics=("arbitrary",),
            vmem_limit_bytes=_vmem_limit(weight_bytes + io_bytes + work_bytes)),
        name="moe",
    )(counts, xs, w["router_wg"], w["router_bg"], w["moe_wgu"], w["moe_wdn"], w["ln3_g"], w["ln3_b"])


def _prepare(p):
    depth = p["w_in"].shape[0]
    row = lambda a: a.reshape(depth, 1, -1)
    w = {k: p[k].astype(BF16) for k in ("w_in", "w_branch", "w_out", "xa_wq", "xa_wkv", "xa_wo")}
    for k in ("b_gate", "pool_scale", "cf_ln_g", "cf_ln_b", "gm_ln_g", "gm_ln_b",
              "ln1_g", "ln1_b", "ln2_g", "ln2_b", "ln3_g", "ln3_b"):
        w[k] = row(p[k])
    for k in ("sc_conv_w", "cf_conv_w", "gm_ws"):
        w[k] = p[k]
    eye = jnp.eye(N_GROUPS, dtype=F32)
    w["pool_w"] = jnp.einsum("lgcd,gh->lgchd", p["pool_w"], eye).reshape(depth, MIX_W, MIX_W).astype(BF16)
    w["gm_bias"] = jnp.repeat(jnp.swapaxes(p["gm_bs"], 1, 2), GROUP_W, axis=2)
    lanes = (jnp.arange(N_EXPERTS) % EXPERTS_PER_GROUP) * ROUTER_LANE_STRIDE + jnp.arange(N_EXPERTS) // EXPERTS_PER_GROUP
    rw = jnp.zeros((D_MODEL, V7X_LANES), F32).at[:, lanes].set(p["router_w"].astype(F32))
    rw_hi = rw.astype(BF16)
    rw_lo = (rw - rw_hi.astype(F32)).astype(BF16)
    w["router_w"] = jnp.concatenate([rw_hi, rw_lo], axis=1)
    w["router_b"] = jnp.zeros((1, V7X_LANES), F32).at[0, lanes].set(p["router_b"].astype(F32))
    slot = jnp.arange(EXPERTS_PER_GROUP) * ROUTER_LANE_STRIDE
    rwg = jnp.zeros((N_EXPERT_GROUPS, D_MODEL, V7X_LANES), F32).at[:, :, slot].set(
        jnp.transpose(p["router_w"].astype(F32).reshape(D_MODEL, N_EXPERT_GROUPS, EXPERTS_PER_GROUP), (1, 0, 2)))
    w["router_wg"] = rwg.astype(BF16)
    w["router_bg"] = jnp.zeros((N_EXPERT_GROUPS, 1, V7X_LANES), F32).at[:, 0, slot].set(
        p["router_b"].astype(F32).reshape(N_EXPERT_GROUPS, EXPERTS_PER_GROUP))
    w["moe_wgu"] = p["moe_w_gu"].reshape(depth * N_EXPERT_GROUPS, EXPERTS_PER_GROUP, D_MODEL, 2 * D_EXPERT)
    w["moe_wdn"] = p["moe_w_down"].reshape(depth * N_EXPERT_GROUPS, EXPERTS_PER_GROUP * D_EXPERT, D_MODEL)
    return w


def kernel(x, mem, w_in, b_gate, pool_w, pool_scale, sc_conv_w, cf_conv_w, cf_ln_g, cf_ln_b, gm_ln_g, gm_ln_b, gm_ws, gm_bs, w_branch, w_out, ln1_g, ln1_b, xa_wq, xa_wkv, xa_wo, ln2_g, ln2_b, router_w, router_b, moe_w_gu, moe_w_down, ln3_g, ln3_b):
    params = dict(w_in=w_in, b_gate=b_gate, pool_w=pool_w, pool_scale=pool_scale, sc_conv_w=sc_conv_w,
                  cf_conv_w=cf_conv_w, cf_ln_g=cf_ln_g, cf_ln_b=cf_ln_b, gm_ln_g=gm_ln_g, gm_ln_b=gm_ln_b,
                  gm_ws=gm_ws, gm_bs=gm_bs, w_branch=w_branch, w_out=w_out, ln1_g=ln1_g, ln1_b=ln1_b,
                  xa_wq=xa_wq, xa_wkv=xa_wkv, xa_wo=xa_wo, ln2_g=ln2_g, ln2_b=ln2_b, router_w=router_w,
                  router_b=router_b, moe_w_gu=moe_w_gu, moe_w_down=moe_w_down, ln3_g=ln3_g, ln3_b=ln3_b)
    w = _prepare(params)
    bsz, seq, d = x.shape
    n = bsz * seq
    k_all, v_all = _kv_call(mem, w["xa_wkv"])
    tile = 512
    assert n <= 1 << RANK_BITS
    sorted_input = None
    for l in range(DEPTH):
        x = _mixer_call(x, l, w, tile, sorted_input)
        x2, code, cnt = _xattn_call(x, k_all, v_all, l, w, tile)
        counts = cnt[0, :N_EXPERT_GROUPS]
        x = _moe_call(_dispatch_call(x2, code, counts, tile), counts, l, w, tile)
        sorted_input = (code, counts, (bsz, seq))
    return _gather_call(x, code, counts, n, tile).reshape(bsz, seq, d)
```

```python
import functools

import jax
import jax.numpy as jnp
from jax import lax
from jax.experimental import pallas as pl
from jax.experimental.pallas import tpu as pltpu

F32 = jnp.float32
BF16 = jnp.bfloat16

D_MODEL = 1024
DEPTH = 4
MIX_W = 256
N_GROUPS = 4
GROUP_W = 64
SHORT_CONV_K = 3
CONFORMER_CONV_K = 31
CHUNK = 128
COL_SCONV = MIX_W
COL_CONF = 4 * MIX_W
COL_GMLP = 6 * MIX_W
COL_GATE = 8 * MIX_W
N_BRANCHES = 4
IN_COLS = COL_GATE + N_BRANCHES * D_MODEL
MEM_LEN = 256
N_XA_HEADS = 4
XA_HEAD_DIM = D_MODEL // N_XA_HEADS
N_EXPERTS = 16
N_EXPERT_GROUPS = 4
EXPERTS_PER_GROUP = 4
D_EXPERT = 256
ALPHA = (2 * DEPTH) ** 0.25
LN_EPS = 1e-5

V7X_SUBLANES = 8
V7X_LANES = 128
V7X_VMEM_BYTES = 64 * 1024 * 1024

HALO = 32
SC_HALO = V7X_SUBLANES
GATE_SLAB = 512
ROUTER_LANE_STRIDE = 32
RANK_BITS = 16
ISSUE_UNROLL = 8
MOE_SUBTILES = 2
XATTN_TILE = 1024
XATTN_BLOCK = 256


def _layer_norm(y, g, b):
    mu = jnp.mean(y, axis=-1, keepdims=True)
    d = y - mu
    var = jnp.mean(d * d, axis=-1, keepdims=True)
    return d * lax.rsqrt(var + LN_EPS) * g + b


def _dot(a, b):
    return jnp.dot(a, b, preferred_element_type=F32)


CHUNKS = D_MODEL // V7X_LANES


def _store_token_major(ref, value, first=0):
    count = value.shape[0]
    for k in range(CHUNKS):
        ref[pl.ds(first * CHUNKS + k, count, stride=CHUNKS), :] = value[:, k * V7X_LANES:(k + 1) * V7X_LANES]


def _load_token_major(ref, first=0, count=None):
    if count is None:
        count = ref.shape[0] // CHUNKS - first
    return jnp.concatenate(
        [ref[pl.ds(first * CHUNKS + k, count, stride=CHUNKS), :] for k in range(CHUNKS)], axis=1)


def _const_spec(shape, index):
    return pl.BlockSpec(shape, lambda *_: index, pipeline_mode=pl.Buffered(1))


def _vmem_limit(nbytes):
    return int(min(V7X_VMEM_BYTES - 4 * 1024 * 1024, nbytes))


def _kv_kernel(mem_ref, wkv_ref, k_ref, v_ref):
    kv = _dot(mem_ref[...].astype(BF16), wkv_ref[...])
    k_ref[...] = kv[:, :D_MODEL].astype(BF16)
    v_ref[...] = kv[:, D_MODEL:].astype(BF16)


def _kv_call(mem, wkv_bf16):
    bsz = mem.shape[0]
    out = jax.ShapeDtypeStruct((DEPTH, bsz, MEM_LEN, D_MODEL), BF16)
    return pl.pallas_call(
        _kv_kernel,
        grid=(DEPTH, bsz),
        in_specs=[
            pl.BlockSpec((None, MEM_LEN, D_MODEL), lambda l, b: (b, 0, 0)),
            pl.BlockSpec((None, D_MODEL, 2 * D_MODEL), lambda l, b: (l, 0, 0)),
        ],
        out_specs=[
            pl.BlockSpec((None, None, MEM_LEN, D_MODEL), lambda l, b: (l, b, 0, 0)),
            pl.BlockSpec((None, None, MEM_LEN, D_MODEL), lambda l, b: (l, b, 0, 0)),
        ],
        out_shape=[out, out],
        compiler_params=pltpu.CompilerParams(dimension_semantics=("arbitrary", "arbitrary")),
        name="mem_kv",
    )(mem, wkv_bf16)


def _prefetched_tokens(cnt_ref, code_first_ref, code_next_ref, xs_ref, buf_ref, first_row_ref, sems, tile):
    step = pl.program_id(0) * pl.num_programs(1) + pl.program_id(1)
    last = pl.num_programs(0) * pl.num_programs(1) - 1
    slot = lax.rem(step, 2)
    firsts, _ = _group_tiles(cnt_ref, tile)
    for g in range(N_EXPERT_GROUPS):
        first_row_ref[g] = firsts[g] * tile

    def token_copy(r, p, to_slot):
        return pltpu.make_async_copy(_token_rows(xs_ref, p), _token_rows(buf_ref.at[to_slot], r), sems.at[to_slot])

    def fetch(code_ref, to_slot):
        def issue(i, carry):
            for j in range(ISSUE_UNROLL):
                r = i * ISSUE_UNROLL + j
                token_copy(r, _sorted_row(code_ref[0, r], first_row_ref), to_slot).start(priority=j % 2)
            return carry

        lax.fori_loop(0, tile // ISSUE_UNROLL, issue, 0)

    @pl.when(step == 0)
    def _():
        fetch(code_first_ref, 0)

    for _ in range(tile):
        token_copy(0, 0, slot).wait()
    x = _load_token_major(buf_ref.at[slot])

    for r in range(tile):
        token_copy(r, _sorted_row(code_next_ref[0, r], first_row_ref), 1 - slot).start(priority=r % 2)

    def drain_last():
        @pl.when(step == last)
        def _():
            for _ in range(tile):
                token_copy(0, 0, 1 - slot).wait()

    return x, drain_last


def _mixer_kernel(*refs, gathered):
    if gathered:
        cnt_ref, code_first_ref, code_next_ref, xs_ref, *refs = refs
        *refs, tok_buf, first_row_ref, sems = refs
    else:
        x_ref, *refs = refs
    (w_in_ref, b_gate_ref, pool_w_ref, pool_scale_ref, sc_w_ref, cf_w_ref,
     cf_g_ref, cf_b_ref, gm_g_ref, gm_b_ref, gm_ws_ref, gm_bias_ref, w_br_ref, w_out_ref,
     ln_g_ref, ln_b_ref, o_ref,
     pool_e, pool_a, pool_b, pool_c, sc_e, cf_e, cf_y, gates, br_a, br_b, br_c, br_d) = refs
    s = pl.program_id(1)
    tile = o_ref.shape[0]
    end = tile + HALO

    @pl.when(s == 0)
    def _():
        pool_e[0:HALO, :] = jnp.zeros((HALO, MIX_W), F32)
        sc_e[0:SC_HALO, :] = jnp.zeros((SC_HALO, MIX_W), F32)
        cf_e[0:HALO, :] = jnp.zeros((HALO, MIX_W), F32)

    if gathered:
        x, drain_last = _prefetched_tokens(
            cnt_ref, code_first_ref, code_next_ref, xs_ref, tok_buf, first_row_ref, sems, tile)
    else:
        x = x_ref[...]
    xb = x.astype(BF16)

    def proj(lo, hi):
        return _dot(xb, w_in_ref[:, lo:hi])

    lane = lax.broadcasted_iota(jnp.int32, (1, MIX_W), 1)
    lane_group = lane // GROUP_W

    z = proj(0, MIX_W)
    pool_e[HALO:end, :] = z
    pb = proj(COL_SCONV, COL_CONF)
    bg = pb[:, MIX_W:2 * MIX_W]
    u = pb[:, 2 * MIX_W:] * pb[:, :MIX_W]
    sc_e[SC_HALO:SC_HALO + tile, :] = u
    pc = proj(COL_CONF, COL_GMLP)
    cf_e[HALO:end, :] = pc[:, :MIX_W] * jax.nn.sigmoid(pc[:, MIX_W:])
    pd = jax.nn.gelu(proj(COL_GMLP, COL_GATE))
    gu = pd[:, :MIX_W]
    gv = _layer_norm(pd[:, MIX_W:], gm_g_ref[...], gm_b_ref[...]).astype(BF16)
    row = lax.broadcasted_iota(jnp.int32, (CHUNK, CHUNK), 0)
    col = lax.broadcasted_iota(jnp.int32, (CHUNK, CHUNK), 1)
    ws_cat = jnp.concatenate(
        [jnp.where(row >= col, gm_ws_ref[g], 0.0).astype(BF16) for g in range(N_GROUPS)], axis=1)

    def pool_unit():
        pool_a[8:end, :] = pool_e[8:end, :] + pool_e[7:end - 1, :]
        pool_b[16:end, :] = pool_a[16:end, :] + pool_a[14:end - 2, :]
        pool_c[24:end, :] = pool_b[24:end, :] + pool_b[20:end - 4, :]
        s16 = pool_c[HALO:end, :] + pool_c[HALO - 8:end - 8, :]
        ssum = jnp.where(lane_group == 0, pool_a[HALO:end, :],
                         jnp.where(lane_group == 1, pool_b[HALO:end, :],
                                   jnp.where(lane_group == 2, pool_c[HALO:end, :], s16)))
        win = jnp.where(lane_group == 0, 2.0,
                        jnp.where(lane_group == 1, 4.0, jnp.where(lane_group == 2, 8.0, 16.0))).astype(F32)
        count = (s * tile + lax.broadcasted_iota(jnp.int32, (tile, 1), 0) + 1).astype(F32)
        a = ssum / jnp.minimum(count, win) - z
        pool_e[0:HALO, :] = pool_e[tile:end, :]
        br_a[...] = (_dot(a.astype(BF16), pool_w_ref[...]) * pool_scale_ref[...]).astype(BF16)

    def sconv_unit():
        conv = (sc_w_ref[0:1, :] * sc_e[SC_HALO - 2:SC_HALO - 2 + tile, :]
                + sc_w_ref[1:2, :] * sc_e[SC_HALO - 1:SC_HALO - 1 + tile, :]
                + sc_w_ref[2:3, :] * u)
        br_b[...] = (bg * conv).astype(BF16)
        sc_e[0:SC_HALO, :] = sc_e[tile:tile + SC_HALO, :]

    first = HALO - (CONFORMER_CONV_K - 1)

    def conv_phase(phase):
        rows = tile if phase == 0 else tile + V7X_SUBLANES
        acc = None
        for q in range((first + CONFORMER_CONV_K - 1) // V7X_SUBLANES + 1):
            k = q * V7X_SUBLANES + phase - first
            if 0 <= k < CONFORMER_CONV_K:
                term = cf_w_ref[k:k + 1, :] * cf_e[q * V7X_SUBLANES:q * V7X_SUBLANES + rows, :]
                acc = term if acc is None else acc + term
        cf_y[phase, 0:rows, :] = acc

    def conv_finish(part, parts):
        n_rows = tile // parts
        lo = part * n_rows
        acc = cf_y[0, lo:lo + n_rows, :]
        for phase in range(1, V7X_SUBLANES):
            acc = acc + cf_y[phase, lo + phase:lo + phase + n_rows, :]
        cn = _layer_norm(acc, cf_g_ref[...], cf_b_ref[...])
        br_c[lo:lo + n_rows, :] = (cn * jax.nn.sigmoid(cn)).astype(BF16)
        if part == parts - 1:
            cf_e[0:HALO, :] = cf_e[tile:end, :]

    def gmlp_unit(c):
        rows = slice(c * CHUNK, (c + 1) * CHUNK)
        vc = gv[rows, :]
        vblk = jnp.concatenate([jnp.where(lane_group == g, vc, jnp.zeros_like(vc)) for g in range(N_GROUPS)],
                               axis=0)
        sv = _dot(ws_cat, vblk) + gm_bias_ref[...]
        br_d[rows, :] = (gu[rows, :] * sv).astype(BF16)

    n_chunks = tile // CHUNK
    units = ([pool_unit, sconv_unit]
             + [functools.partial(conv_phase, p) for p in range(V7X_SUBLANES)]
             + [functools.partial(gmlp_unit, c) for c in range(n_chunks)]
             + [functools.partial(conv_finish, p, 4) for p in range(4)])
    n_slabs = N_BRANCHES * D_MODEL // GATE_SLAB
    per_slab = -(-len(units) // n_slabs)
    for r in range(n_slabs):
        lo = r * GATE_SLAB
        gates[:, lo:lo + GATE_SLAB] = jax.nn.sigmoid(
            proj(COL_GATE + lo, COL_GATE + lo + GATE_SLAB) + b_gate_ref[:, lo:lo + GATE_SLAB])
        for unit in units[r * per_slab:(r + 1) * per_slab]:
            unit()

    merged = None
    for i, br in enumerate((br_a, br_b, br_c, br_d)):
        term = gates[:, i * D_MODEL:(i + 1) * D_MODEL] * _dot(br[...], w_br_ref[i])
        merged = term if merged is None else merged + term
    mix = _dot(merged.astype(BF16), w_out_ref[...])
    o_ref[...] = _layer_norm(ALPHA * x + mix, ln_g_ref[...], ln_b_ref[...])
    if gathered:
        drain_last()


def _mixer_call(x, l, w, tile, sorted_input=None):
    gathered = sorted_input is not None
    if gathered:
        code, counts, (bsz, seq) = sorted_input
    else:
        bsz, seq, _ = x.shape
    steps = seq // tile
    rows = tile + HALO
    vec = lambda n: _const_spec((None, 1, n), (l, 0, 0))
    in_specs = [
        _const_spec((None, D_MODEL, IN_COLS), (l, 0, 0)),
        vec(N_BRANCHES * D_MODEL),
        _const_spec((None, MIX_W, MIX_W), (l, 0, 0)),
        vec(MIX_W),
        _const_spec((None, SHORT_CONV_K, MIX_W), (l, 0, 0)),
        _const_spec((None, CONFORMER_CONV_K, MIX_W), (l, 0, 0)),
        vec(MIX_W), vec(MIX_W), vec(MIX_W), vec(MIX_W),
        _const_spec((None, N_GROUPS, CHUNK, CHUNK), (l, 0, 0, 0)),
        _const_spec((None, CHUNK, MIX_W), (l, 0, 0)),
        _const_spec((None, N_BRANCHES, MIX_W, D_MODEL), (l, 0, 0, 0)),
        _const_spec((None, D_MODEL, D_MODEL), (l, 0, 0)),
        vec(D_MODEL), vec(D_MODEL),
    ]
    scratch = [
        pltpu.VMEM((rows, MIX_W), F32), pltpu.VMEM((rows, MIX_W), F32),
        pltpu.VMEM((rows, MIX_W), F32), pltpu.VMEM((rows, MIX_W), F32),
        pltpu.VMEM((tile + 2 * SC_HALO, MIX_W), F32),
        pltpu.VMEM((rows, MIX_W), F32),
        pltpu.VMEM((V7X_SUBLANES, tile + V7X_SUBLANES, MIX_W), F32),
        pltpu.VMEM((tile, N_BRANCHES * D_MODEL), F32),
        pltpu.VMEM((tile, MIX_W), BF16), pltpu.VMEM((tile, MIX_W), BF16),
        pltpu.VMEM((tile, MIX_W), BF16), pltpu.VMEM((tile, MIX_W), BF16),
    ]
    weights = (w["w_in"], w["b_gate"], w["pool_w"], w["pool_scale"], w["sc_conv_w"], w["cf_conv_w"],
               w["cf_ln_g"], w["cf_ln_b"], w["gm_ln_g"], w["gm_ln_b"], w["gm_ws"], w["gm_bias"],
               w["w_branch"], w["w_out"], w["ln1_g"], w["ln1_b"])
    if gathered:
        last = bsz * steps - 1
        in_specs = [
            pl.BlockSpec((None, 1, tile), lambda b, s, cnt: (0, 0, 0), memory_space=pltpu.SMEM),
            pl.BlockSpec((None, 1, tile), lambda b, s, cnt: (jnp.minimum(b * steps + s + 1, last), 0, 0),
                         memory_space=pltpu.SMEM),
            pl.BlockSpec(memory_space=pl.ANY),
        ] + in_specs
        scratch = scratch + [pltpu.VMEM((2, tile * CHUNKS, V7X_LANES), F32),
                             pltpu.SMEM((N_EXPERT_GROUPS,), jnp.int32),
                             pltpu.SemaphoreType.DMA((2,))]
        operands = (counts, code, code, x) + weights
    else:
        in_specs = [pl.BlockSpec((None, tile, D_MODEL), lambda b, s: (b, s, 0))] + in_specs
        operands = (x,) + weights
    weight_bytes = 2 * (D_MODEL * IN_COLS + N_BRANCHES * MIX_W * D_MODEL + D_MODEL * D_MODEL + MIX_W * MIX_W)
    io_bytes = 2 * 2 * tile * D_MODEL * 4
    work_bytes = 20 * tile * D_MODEL * 4
    return pl.pallas_call(
        functools.partial(_mixer_kernel, gathered=gathered),
        grid_spec=pltpu.PrefetchScalarGridSpec(
            num_scalar_prefetch=1 if gathered else 0,
            grid=(bsz, steps),
            in_specs=in_specs,
            out_specs=pl.BlockSpec((None, tile, D_MODEL), lambda b, s, *_: (b, s, 0)),
            scratch_shapes=scratch,
        ),
        out_shape=jax.ShapeDtypeStruct((bsz, seq, D_MODEL), F32),
        compiler_params=pltpu.CompilerParams(
            dimension_semantics=("arbitrary", "arbitrary"),
            vmem_limit_bytes=_vmem_limit(weight_bytes + io_bytes + work_bytes)),
        name="mixer",
    )(*operands)


def _router_scores(x, rw_ref, rb_ref):
    x_hi = x.astype(BF16)
    x_lo = (x - x_hi.astype(F32)).astype(BF16)
    by_hi = _dot(x_hi, rw_ref[...])
    by_lo = _dot(x_lo, rw_ref[:, :V7X_LANES])
    return jax.nn.sigmoid(by_hi[:, :V7X_LANES] + by_hi[:, V7X_LANES:] + by_lo + rb_ref[...])


def _expert_slots(scores):
    return [scores if j == 0 else pltpu.roll(scores, V7X_LANES - j * ROUTER_LANE_STRIDE, axis=1)
            for j in range(EXPERTS_PER_GROUP)]


def _top2_weights(sj, keep):
    picked = []
    for j in range(EXPERTS_PER_GROUP):
        rank = jnp.zeros_like(sj[j])
        for i in range(EXPERTS_PER_GROUP):
            if i < j:
                rank = rank + (sj[i] >= sj[j]).astype(F32)
            elif i > j:
                rank = rank + (sj[i] > sj[j]).astype(F32)
        picked.append(jnp.where(keep & (rank < 2.0), sj[j], 0.0))
    denom = jnp.sum(picked[0] + picked[1] + picked[2] + picked[3], axis=-1, keepdims=True)
    return [p / denom for p in picked]


def _xattn_kernel(x_ref, k_ref, v_ref, wq_ref, wo_ref, ln_g_ref, ln_b_ref, rw_ref, rb_ref,
                  o_ref, pos_ref, cnt_ref, base_ref):
    @pl.when((pl.program_id(0) == 0) & (pl.program_id(1) == 0))
    def _():
        base_ref[...] = jnp.zeros_like(base_ref)

    tile = x_ref.shape[0]
    sub = XATTN_BLOCK
    n_blocks = tile // sub
    lane = lax.broadcasted_iota(jnp.int32, (1, V7X_LANES), 1)
    r = lax.broadcasted_iota(jnp.int32, (sub, sub), 0)
    c = lax.broadcasted_iota(jnp.int32, (sub, sub), 1)
    before = (r > c).astype(BF16)
    head = lambda h: slice(h * XA_HEAD_DIM, (h + 1) * XA_HEAD_DIM)

    def scores(part):
        x = x_ref[part * sub:(part + 1) * sub, :]
        q = _dot(x.astype(BF16), wq_ref[...]).astype(BF16)
        return x, [lax.dot_general(q[:, head(h)], k_ref[:, head(h)], (((1,), (1,)), ((), ())),
                                   preferred_element_type=F32) * (XA_HEAD_DIM ** -0.5)
                   for h in range(N_XA_HEADS)]

    def softmax(sc):
        e = jnp.exp(sc - jnp.max(sc, axis=-1, keepdims=True))
        return (e / jnp.sum(e, axis=-1, keepdims=True)).astype(BF16)

    def attend(part, x, probs):
        o = jnp.concatenate([_dot(probs[h], v_ref[:, head(h)]).astype(BF16) for h in range(N_XA_HEADS)], axis=-1)
        x2 = _layer_norm(ALPHA * x + _dot(o, wo_ref[...]), ln_g_ref[...], ln_b_ref[...])
        _store_token_major(o_ref, x2, part * sub)
        return _router_scores(x2, rw_ref, rb_ref)

    def select_group(scores_):
        sj = _expert_slots(scores_)
        hi01, lo01 = jnp.maximum(sj[0], sj[1]), jnp.minimum(sj[0], sj[1])
        hi23, lo23 = jnp.maximum(sj[2], sj[3]), jnp.minimum(sj[2], sj[3])
        top1 = jnp.maximum(hi01, hi23)
        top2 = jnp.maximum(jnp.minimum(hi01, hi23), jnp.maximum(lo01, lo23))
        gscore = jnp.where(lane < N_EXPERT_GROUPS, top1 + top2, -1.0)
        gmax = jnp.max(gscore, axis=-1, keepdims=True)
        sel = jnp.min(jnp.where(gscore == gmax, lane, V7X_LANES), axis=-1, keepdims=True)
        return sel, (lane == sel).astype(F32)

    def place(part, sel, onehot, base):
        earlier = _dot(before, onehot.astype(BF16))
        rank = jnp.sum(onehot * (earlier + base), axis=-1, keepdims=True)
        code = sel.astype(F32) * float(1 << RANK_BITS) + rank
        code_rows = jnp.transpose(jnp.broadcast_to(code, (sub, V7X_LANES)))
        pos_ref[:, part * sub:(part + 1) * sub] = code_rows[0:1, :].astype(jnp.int32)
        return base + jnp.sum(onehot, axis=0, keepdims=True)

    base = base_ref[...]
    stage1, stage2, stage3 = {}, {}, {}
    for t in range(n_blocks + 3):
        if t < n_blocks:
            stage1[t] = scores(t)
        if 0 <= t - 1 < n_blocks:
            x, sc = stage1.pop(t - 1)
            stage2[t - 1] = attend(t - 1, x, [softmax(s) for s in sc])
        if 0 <= t - 2 < n_blocks:
            stage3[t - 2] = select_group(stage2.pop(t - 2))
        if 0 <= t - 3 < n_blocks:
            base = place(t - 3, *stage3.pop(t - 3), base)
    base_ref[...] = base
    cnt_ref[...] = jnp.broadcast_to(base, (V7X_SUBLANES, V7X_LANES)).astype(jnp.int32)


def _xattn_call(x, k, v, l, w, tile):
    bsz, seq, _ = x.shape
    steps = seq // tile
    vec = lambda n: _const_spec((None, 1, n), (l, 0, 0))
    weight_bytes = 2 * 2 * D_MODEL * D_MODEL + 2 * 2 * 2 * MEM_LEN * D_MODEL
    io_bytes = 2 * 2 * tile * D_MODEL * 4
    work_bytes = 10 * tile * D_MODEL * 4
    return pl.pallas_call(
        _xattn_kernel,
        grid=(bsz, steps),
        in_specs=[
            pl.BlockSpec((None, tile, D_MODEL), lambda b, s: (b, s, 0)),
            pl.BlockSpec((None, None, MEM_LEN, D_MODEL), lambda b, s: (l, b, 0, 0)),
            pl.BlockSpec((None, None, MEM_LEN, D_MODEL), lambda b, s: (l, b, 0, 0)),
            _const_spec((None, D_MODEL, D_MODEL), (l, 0, 0)),
            _const_spec((None, D_MODEL, D_MODEL), (l, 0, 0)),
            vec(D_MODEL), vec(D_MODEL),
            _const_spec((D_MODEL, 2 * V7X_LANES), (0, 0)),
            _const_spec((1, V7X_LANES), (0, 0)),
        ],
        out_specs=[
            pl.BlockSpec((tile * CHUNKS, V7X_LANES), lambda b, s: (b * steps + s, 0)),
            pl.BlockSpec((None, 1, tile), lambda b, s: (b * steps + s, 0, 0)),
            pl.BlockSpec((V7X_SUBLANES, V7X_LANES), lambda b, s: (0, 0)),
        ],
        out_shape=[
            jax.ShapeDtypeStruct((bsz * seq * CHUNKS, V7X_LANES), F32),
            jax.ShapeDtypeStruct((bsz * steps, 1, tile), jnp.int32),
            jax.ShapeDtypeStruct((V7X_SUBLANES, V7X_LANES), jnp.int32),
        ],
        scratch_shapes=[pltpu.VMEM((1, V7X_LANES), F32)],
        compiler_params=pltpu.CompilerParams(
            dimension_semantics=("arbitrary", "arbitrary"),
            vmem_limit_bytes=_vmem_limit(weight_bytes + io_bytes + work_bytes)),
        name="xattn",
    )(x, k, v, w["xa_wq"], w["xa_wo"], w["ln2_g"], w["ln2_b"], w["router_w"], w["router_b"])


def _token_rows(ref, t):
    return ref.at[pl.ds(pl.multiple_of(t * CHUNKS, CHUNKS), CHUNKS)]


def _group_tiles(cnt_ref, tile):
    shift = tile.bit_length() - 1
    assert tile == 1 << shift
    firsts, total = [], jnp.int32(0)
    for g in range(N_EXPERT_GROUPS):
        firsts.append(total)
        total = total + lax.shift_right_logical(cnt_ref[g] + (tile - 1), shift)
    return firsts, total


def _sorted_row(code, first_row_ref):
    group = lax.shift_right_logical(code, RANK_BITS)
    return first_row_ref[group] + (code & ((1 << RANK_BITS) - 1))


def _dispatch_kernel(cnt_ref, code_ref, x_ref, xs_ref, zero_ref, first_row_ref, sem):
    tile = x_ref.shape[0] // CHUNKS
    tile_rows = tile * CHUNKS
    n_tiles = xs_ref.shape[0] // tile_rows
    firsts, total = _group_tiles(cnt_ref, tile)
    for g in range(N_EXPERT_GROUPS):
        first_row_ref[g] = firsts[g] * tile

    def clear_tile(t):
        fill = pltpu.make_async_copy(
            zero_ref, xs_ref.at[pl.ds(pl.multiple_of(t * tile_rows, tile_rows), tile_rows)], sem)
        fill.start()
        fill.wait()

    @pl.when(pl.program_id(0) == 0)
    def _():
        zero_ref[...] = jnp.zeros_like(zero_ref)
        ends = firsts[1:] + [total]
        for g in range(N_EXPERT_GROUPS):
            @pl.when(cnt_ref[g] > 0)
            def _():
                clear_tile(ends[g] - 1)
        for t in range(n_tiles - N_EXPERT_GROUPS, n_tiles):
            @pl.when(total <= t)
            def _():
                clear_tile(t)

    def token_copy(r, p):
        return pltpu.make_async_copy(_token_rows(x_ref, r), _token_rows(xs_ref, p), sem)

    def issue(i, carry):
        for j in range(ISSUE_UNROLL):
            r = i * ISSUE_UNROLL + j
            token_copy(r, _sorted_row(code_ref[0, r], first_row_ref)).start(priority=j % 2)
        return carry

    lax.fori_loop(0, tile // ISSUE_UNROLL, issue, 0)
    for _ in range(tile):
        token_copy(0, 0).wait()


def _dispatch_call(x_tm, code, counts, tile):
    n = x_tm.shape[0] // CHUNKS
    n_tiles = n // tile + N_EXPERT_GROUPS
    return pl.pallas_call(
        _dispatch_kernel,
        grid_spec=pltpu.PrefetchScalarGridSpec(
            num_scalar_prefetch=1,
            grid=(n // tile,),
            in_specs=[
                pl.BlockSpec((None, 1, tile), lambda t, cnt: (t, 0, 0), memory_space=pltpu.SMEM),
                pl.BlockSpec((tile * CHUNKS, V7X_LANES), lambda t, cnt: (t, 0)),
            ],
            out_specs=pl.BlockSpec(memory_space=pl.ANY),
            scratch_shapes=[pltpu.VMEM((tile * CHUNKS, V7X_LANES), F32),
                            pltpu.SMEM((N_EXPERT_GROUPS,), jnp.int32),
                            pltpu.SemaphoreType.DMA],
        ),
        out_shape=jax.ShapeDtypeStruct((n_tiles * tile * CHUNKS, V7X_LANES), F32),
        compiler_params=pltpu.CompilerParams(
            dimension_semantics=("arbitrary",),
            vmem_limit_bytes=_vmem_limit(4 * tile * D_MODEL * 4 + (1 << 20))),
        name="dispatch",
    )(counts, code, x_tm)


def _gather_kernel(cnt_ref, code_ref, xs_ref, o_ref, buf_ref, first_row_ref, sem):
    tile = o_ref.shape[0]
    firsts, _ = _group_tiles(cnt_ref, tile)
    for g in range(N_EXPERT_GROUPS):
        first_row_ref[g] = firsts[g] * tile

    def token_copy(r, p):
        return pltpu.make_async_copy(_token_rows(xs_ref, p), _token_rows(buf_ref, r), sem)

    def issue(i, carry):
        for j in range(ISSUE_UNROLL):
            r = i * ISSUE_UNROLL + j
            token_copy(r, _sorted_row(code_ref[0, r], first_row_ref)).start(priority=j % 2)
        return carry

    lax.fori_loop(0, tile // ISSUE_UNROLL, issue, 0)
    for _ in range(tile):
        token_copy(0, 0).wait()
    o_ref[...] = _load_token_major(buf_ref)


def _gather_call(xs, code, counts, n, tile):
    return pl.pallas_call(
        _gather_kernel,
        grid_spec=pltpu.PrefetchScalarGridSpec(
            num_scalar_prefetch=1,
            grid=(n // tile,),
            in_specs=[
                pl.BlockSpec((None, 1, tile), lambda t, cnt: (t, 0, 0), memory_space=pltpu.SMEM),
                pl.BlockSpec(memory_space=pl.ANY),
            ],
            out_specs=pl.BlockSpec((tile, D_MODEL), lambda t, cnt: (t, 0)),
            scratch_shapes=[pltpu.VMEM((tile * CHUNKS, V7X_LANES), F32),
                            pltpu.SMEM((N_EXPERT_GROUPS,), jnp.int32),
                            pltpu.SemaphoreType.DMA],
        ),
        out_shape=jax.ShapeDtypeStruct((n, D_MODEL), F32),
        compiler_params=pltpu.CompilerParams(
            dimension_semantics=("arbitrary",),
            vmem_limit_bytes=_vmem_limit(4 * tile * D_MODEL * 4 + (1 << 20))),
        name="gather",
    )(counts, code, xs)


def _tile_group(step, cnt_ref, tile):
    firsts, total = _group_tiles(cnt_ref, tile)
    group = jnp.int32(0)
    for g in range(1, N_EXPERT_GROUPS):
        group = group + (step >= firsts[g]).astype(jnp.int32)
    return group, step < total


def _moe_kernel(cnt_ref, x_ref, rw_ref, rb_ref, wgu_f32_ref, wdn_f32_ref, ln_g_ref, ln_b_ref, o_ref,
                wgu_ref, wdn_ref):
    tile = x_ref.shape[0] // CHUNKS
    step = pl.program_id(0)
    group, valid = _tile_group(step, cnt_ref, tile)
    previous, _ = _tile_group(step - 1, cnt_ref, tile)

    @pl.when((step == 0) | (group != previous))
    def _():
        for j in range(EXPERTS_PER_GROUP):
            wgu_ref[j] = wgu_f32_ref[j].astype(BF16)
        wdn_ref[...] = wdn_f32_ref[...].astype(BF16)

    @pl.when(jnp.logical_not(valid))
    def _():
        o_ref[...] = jnp.zeros_like(o_ref)

    @pl.when(valid)
    def _():
        lane = lax.broadcasted_iota(jnp.int32, (1, V7X_LANES), 1)
        sub = tile // MOE_SUBTILES
        def hidden(part):
            x = _load_token_major(x_ref, part * sub, sub)
            xb = x.astype(BF16)
            sj = _expert_slots(jax.nn.sigmoid(_dot(xb, rw_ref[...]) + rb_ref[...]))
            comb = _top2_weights(sj, lane == 0)
            hs = []
            for j in range(EXPERTS_PER_GROUP):
                gu = _dot(xb, wgu_ref[j])
                gate, up = gu[:, :D_EXPERT], gu[:, D_EXPERT:]
                hs.append((gate * jax.nn.sigmoid(gate) * up * comb[j][:, 0:1]).astype(BF16))
            return x, jnp.concatenate(hs, axis=-1)

        def finish(part, x, h):
            y = _dot(h, wdn_ref[...])
            _store_token_major(o_ref, _layer_norm(ALPHA * x + y, ln_g_ref[...], ln_b_ref[...]), part * sub)

        pending = {}
        for t in range(MOE_SUBTILES + 1):
            if t < MOE_SUBTILES:
                pending[t] = hidden(t)
            if t >= 1:
                finish(t - 1, *pending.pop(t - 1))


def _moe_call(xs, counts, l, w, tile):
    hid = EXPERTS_PER_GROUP * D_EXPERT
    n_tiles = xs.shape[0] // (tile * CHUNKS)

    def group_of(t, cnt):
        return _tile_group(t, cnt, tile)[0]

    vec = lambda m: pl.BlockSpec((None, 1, m), lambda t, cnt: (l, 0, 0), pipeline_mode=pl.Buffered(1))
    group_weights = D_MODEL * 2 * hid + hid * D_MODEL
    weight_bytes = 2 * 4 * group_weights + 2 * group_weights
    io_bytes = 2 * 2 * tile * D_MODEL * 4
    work_bytes = 10 * tile * D_MODEL * 4
    return pl.pallas_call(
        _moe_kernel,
        grid_spec=pltpu.PrefetchScalarGridSpec(
            num_scalar_prefetch=1,
            grid=(n_tiles,),
            in_specs=[
                pl.BlockSpec((tile * CHUNKS, V7X_LANES), lambda t, cnt: (t, 0)),
                pl.BlockSpec((None, D_MODEL, V7X_LANES), lambda t, cnt: (group_of(t, cnt), 0, 0)),
                pl.BlockSpec((None, 1, V7X_LANES), lambda t, cnt: (group_of(t, cnt), 0, 0)),
                pl.BlockSpec((None, EXPERTS_PER_GROUP, D_MODEL, 2 * D_EXPERT),
                             lambda t, cnt: (l * N_EXPERT_GROUPS + group_of(t, cnt), 0, 0, 0)),
                pl.BlockSpec((None, hid, D_MODEL),
                             lambda t, cnt: (l * N_EXPERT_GROUPS + group_of(t, cnt), 0, 0)),
                vec(D_MODEL), vec(D_MODEL),
            ],
            out_specs=pl.BlockSpec((tile * CHUNKS, V7X_LANES), lambda t, cnt: (t, 0)),
            scratch_shapes=[pltpu.VMEM((EXPERTS_PER_GROUP, D_MODEL, 2 * D_EXPERT), BF16),
                            pltpu.VMEM((hid, D_MODEL), BF16)],
        ),
        out_shape=jax.ShapeDtypeStruct(xs.shape, F32),
        compiler_params=pltpu.CompilerParams(
            dimension_semantics=("arbitrary",),
            vmem_limit_bytes=_vmem_limit(weight_bytes + io_bytes + work_bytes)),
        name="moe",
    )(counts, xs, w["router_wg"], w["router_bg"], w["moe_wgu"], w["moe_wdn"], w["ln3_g"], w["ln3_b"])


def _prepare(p):
    depth = p["w_in"].shape[0]
    row = lambda a: a.reshape(depth, 1, -1)
    w = {k: p[k].astype(BF16) for k in ("w_in", "w_branch", "w_out", "xa_wq", "xa_wkv", "xa_wo")}
    for k in ("b_gate", "pool_scale", "cf_ln_g", "cf_ln_b", "gm_ln_g", "gm_ln_b",
              "ln1_g", "ln1_b", "ln2_g", "ln2_b", "ln3_g", "ln3_b"):
        w[k] = row(p[k])
    for k in ("sc_conv_w", "cf_conv_w", "gm_ws"):
        w[k] = p[k]
    eye = jnp.eye(N_GROUPS, dtype=F32)
    w["pool_w"] = jnp.einsum("lgcd,gh->lgchd", p["pool_w"], eye).reshape(depth, MIX_W, MIX_W).astype(BF16)
    w["gm_bias"] = jnp.repeat(jnp.swapaxes(p["gm_bs"], 1, 2), GROUP_W, axis=2)
    lanes = (jnp.arange(N_EXPERTS) % EXPERTS_PER_GROUP) * ROUTER_LANE_STRIDE + jnp.arange(N_EXPERTS) // EXPERTS_PER_GROUP
    rw = jnp.zeros((D_MODEL, V7X_LANES), F32).at[:, lanes].set(p["router_w"].astype(F32))
    rw_hi = rw.astype(BF16)
    rw_lo = (rw - rw_hi.astype(F32)).astype(BF16)
    w["router_w"] = jnp.concatenate([rw_hi, rw_lo], axis=1)
    w["router_b"] = jnp.zeros((1, V7X_LANES), F32).at[0, lanes].set(p["router_b"].astype(F32))
    slot = jnp.arange(EXPERTS_PER_GROUP) * ROUTER_LANE_STRIDE
    rwg = jnp.zeros((N_EXPERT_GROUPS, D_MODEL, V7X_LANES), F32).at[:, :, slot].set(
        jnp.transpose(p["router_w"].astype(F32).reshape(D_MODEL, N_EXPERT_GROUPS, EXPERTS_PER_GROUP), (1, 0, 2)))
    w["router_wg"] = rwg.astype(BF16)
    w["router_bg"] = jnp.zeros((N_EXPERT_GROUPS, 1, V7X_LANES), F32).at[:, 0, slot].set(
        p["router_b"].astype(F32).reshape(N_EXPERT_GROUPS, EXPERTS_PER_GROUP))
    w["moe_wgu"] = p["moe_w_gu"].reshape(depth * N_EXPERT_GROUPS, EXPERTS_PER_GROUP, D_MODEL, 2 * D_EXPERT)
    w["moe_wdn"] = p["moe_w_down"].reshape(depth * N_EXPERT_GROUPS, EXPERTS_PER_GROUP * D_EXPERT, D_MODEL)
    return w


def kernel(x, mem, w_in, b_gate, pool_w, pool_scale, sc_conv_w, cf_conv_w, cf_ln_g, cf_ln_b, gm_ln_g, gm_ln_b, gm_ws, gm_bs, w_branch, w_out, ln1_g, ln1_b, xa_wq, xa_wkv, xa_wo, ln2_g, ln2_b, router_w, router_b, moe_w_gu, moe_w_down, ln3_g, ln3_b):
    params = dict(w_in=w_in, b_gate=b_gate, pool_w=pool_w, pool_scale=pool_scale, sc_conv_w=sc_conv_w,
                  cf_conv_w=cf_conv_w, cf_ln_g=cf_ln_g, cf_ln_b=cf_ln_b, gm_ln_g=gm_ln_g, gm_ln_b=gm_ln_b,
                  gm_ws=gm_ws, gm_bs=gm_bs, w_branch=w_branch, w_out=w_out, ln1_g=ln1_g, ln1_b=ln1_b,
                  xa_wq=xa_wq, xa_wkv=xa_wkv, xa_wo=xa_wo, ln2_g=ln2_g, ln2_b=ln2_b, router_w=router_w,
                  router_b=router_b, moe_w_gu=moe_w_gu, moe_w_down=moe_w_down, ln3_g=ln3_g, ln3_b=ln3_b)
    w = _prepare(params)
    bsz, seq, d = x.shape
    n = bsz * seq
    k_all, v_all = _kv_call(mem, w["xa_wkv"])
    tile = 512
    assert n <= 1 << RANK_BITS
    sorted_input = None
    for l in range(DEPTH):
        x = _mixer_call(x, l, w, tile, sorted_input)
        x2, code, cnt = _xattn_call(x, k_all, v_all, l, w, XATTN_TILE)
        code = code.reshape(n // tile, 1, tile)
        counts = cnt[0, :N_EXPERT_GROUPS]
        x = _moe_call(_dispatch_call(x2, code, counts, tile), counts, l, w, tile)
        sorted_input = (code, counts, (bsz, seq))
    return _gather_call(x, code, counts, n, tile).reshape(bsz, seq, d)
```
